```python
import math, functools
import jax, jax.numpy as jnp
from jax import lax
import numpy as np

D_MODEL = 1024
BATCH = 32
SEQ = 2048
DEPTH = 1
DEC_BATCH = 128
DEC_SEQ = 1
PAST_LEN = 8192
PAGE_SIZE = 128

DA_HEADS = 8
DA_QK = 64
DA_V = 2 * DA_QK
DA_Q_W = DA_HEADS * 2 * DA_QK
DA_V_W = DA_HEADS * DA_V
ROPE_THETA = 10000.0
Q_BLOCK = 128
RW_HEAD = 64
RW_HEADS = D_MODEL // RW_HEAD
RW_D = RW_HEADS * RW_HEAD
DECAY_LORA = 64
AAA_LORA = 64
GATE_LORA = 160
RW_W = 3 * RW_D + DECAY_LORA + AAA_LORA + GATE_LORA
GATE_W = 2 * D_MODEL
O_Q = 0
O_K = DA_Q_W
O_V = 2 * DA_Q_W
O_G = O_V + DA_V_W
O_RW = O_G + GATE_W
IN_W = O_RW + RW_W
D_FF = 4 * D_MODEL
ALPHA = (2 * DEPTH) ** 0.25
BETA = (8 * DEPTH) ** -0.25
LN_EPS = 1e-5
SUBLN_EPS = 1e-5
GN_EPS = 64e-5

kernel_name = 'diffattn_rwkv7_gated_hybrid_step'


def _layer_norm(x, g, b):
    xf = x.astype(jnp.float32)
    mu = xf.mean(-1, keepdims=True)
    var = jnp.square(xf - mu).mean(-1, keepdims=True)
    return ((xf - mu) * lax.rsqrt(var + LN_EPS) * g + b).astype(x.dtype)


def _rope(t, pos):
    half = DA_QK // 2
    inv = ROPE_THETA ** (-jnp.arange(half, dtype=jnp.float32) / half)
    ang = pos.astype(jnp.float32)[:, None] * inv[None, :]
    cos = jnp.cos(ang)[:, None, None, :]
    sin = jnp.sin(ang)[:, None, None, :]
    tf = t.astype(jnp.float32)
    t1, t2 = tf[..., :half], tf[..., half:]
    return jnp.concatenate([t1 * cos - t2 * sin, t2 * cos + t1 * sin], -1).astype(t.dtype)


def _diff_attn_prompt(q, k, v, lam):
    B, T = q.shape[:2]
    nqb = T // Q_BLOCK
    qb = q.reshape(B, nqb, Q_BLOCK, DA_HEADS, 2, DA_QK).swapaxes(0, 1)
    kpos = jnp.arange(T)
    scale = DA_QK ** -0.5
    vf = v.astype(jnp.float32)

    def block(args):
        i, qi = args
        s = jnp.einsum('bqhcd,bkhcd->bhcqk', qi, k, preferred_element_type=jnp.float32) * scale
        qpos = i * Q_BLOCK + jnp.arange(Q_BLOCK)
        s = jnp.where(kpos[None, :] <= qpos[:, None], s, -jnp.inf)
        p = jax.nn.softmax(s, axis=-1)
        wts = p[:, :, 0] - lam * p[:, :, 1]
        return jnp.einsum('bhqk,bkhd->bqhd', wts, vf)

    o = lax.map(block, (jnp.arange(nqb), qb))
    return o.swapaxes(0, 1).reshape(B, T, DA_HEADS, DA_V)


def _diff_attn_decode(q, k, v, lam, cache_k, cache_v, page_table):
    Bd, Td = q.shape[:2]
    qf = q.astype(jnp.float32) * (DA_QK ** -0.5)
    s = jnp.einsum('bqhcd,bkhcd->bhcqk', qf, k.astype(jnp.float32))
    causal = jnp.arange(Td)[None, :] <= jnp.arange(Td)[:, None]
    s = jnp.where(causal, s, -jnp.inf)
    m = s.max(-1)
    pe = jnp.exp(s - m[..., None])
    l = pe.sum(-1)
    acc = jnp.einsum('bhcqk,bkhd->bhcqd', pe, v.astype(jnp.float32))

    def step(carry, pid):
        m, l, acc = carry
        kp = cache_k[pid].reshape(Bd, PAGE_SIZE, DA_HEADS, 2, DA_QK).astype(jnp.float32)
        vp = cache_v[pid].astype(jnp.float32)
        sp = jnp.einsum('bqhcd,bphcd->bhcqp', qf, kp)
        m_new = jnp.maximum(m, sp.max(-1))
        corr = jnp.exp(m - m_new)
        pp = jnp.exp(sp - m_new[..., None])
        l = l * corr + pp.sum(-1)
        acc = acc * corr[..., None] + jnp.einsum('bhcqp,bphd->bhcqd', pp, vp)
        return (m_new, l, acc), None

    (m, l, acc), _ = lax.scan(step, (m, l, acc), page_table.T)
    o = acc / l[..., None]
    o = o[:, :, 0] - lam * o[:, :, 1]
    return o.transpose(0, 2, 1, 3)


def _da_subnorm(o, g, lam_init):
    B, T = o.shape[:2]
    o = o * lax.rsqrt(jnp.mean(jnp.square(o), -1, keepdims=True) + SUBLN_EPS)
    o = o * g.reshape(DA_HEADS, DA_V) * (1.0 - lam_init)
    return o.reshape(B, T, DA_V_W)


def _wkv_scan(wkv0, r, decay, k, v, kk, a):
    xs = tuple(t.swapaxes(0, 1) for t in (r, decay, k, v, kk, a))

    def step(S, inp):
        r_t, w_t, k_t, v_t, kk_t, a_t = inp
        sa = jnp.einsum('bhvk,bhk->bhv', S, -kk_t)
        S = (S * w_t[:, :, None, :] + sa[..., None] * (kk_t * a_t)[:, :, None, :]
             + v_t[..., None] * k_t[:, :, None, :])
        y = jnp.einsum('bhvk,bhk->bhv', S, r_t)
        return S, y

    S, y = lax.scan(step, wkv0.astype(jnp.float32), xs)
    return S, y.swapaxes(0, 1)


def _rwkv_branch(p_rw, shift0, wkv0, lp):
    B, T, _ = p_rw.shape
    f32 = jnp.float32
    prev = jnp.concatenate([shift0[:, None, :].astype(p_rw.dtype), p_rw[:, :-1]], axis=1)
    xm = (p_rw + (prev - p_rw) * lp['rw_mu']).astype(f32)
    r = xm[..., :RW_D]
    k = xm[..., RW_D:2 * RW_D]
    v = xm[..., 2 * RW_D:3 * RW_D]
    wd = xm[..., 3 * RW_D:3 * RW_D + DECAY_LORA]
    ad = xm[..., 3 * RW_D + DECAY_LORA:3 * RW_D + DECAY_LORA + AAA_LORA]
    gd = xm[..., 3 * RW_D + DECAY_LORA + AAA_LORA:]
    w_log = -jax.nn.softplus(-(lp['rw_w0'].astype(f32) + jnp.tanh(wd) @ lp['rw_w2'].astype(f32))) - 0.5
    decay = jnp.exp(-jnp.exp(w_log))
    a = jax.nn.sigmoid(lp['rw_a0'].astype(f32) + ad @ lp['rw_a2'].astype(f32))
    g = jax.nn.sigmoid(gd) @ lp['rw_g2'].astype(f32)
    kk = k * lp['rw_k_k'].astype(f32)
    k = k * (1.0 + (a - 1.0) * lp['rw_k_a'].astype(f32))
    hs = (B, T, RW_HEADS, RW_HEAD)
    r, k, v, kk, a, decay = (t.reshape(hs) for t in (r, k, v, kk, a, decay))
    kk = kk / jnp.maximum(jnp.sqrt(jnp.sum(jnp.square(kk), -1, keepdims=True)), 1e-12)
    wkv, y = _wkv_scan(wkv0, r, decay, k, v, kk, a)
    mu = y.mean(-1, keepdims=True)
    var = jnp.square(y - mu).mean(-1, keepdims=True)
    yn = ((y - mu) * lax.rsqrt(var + GN_EPS)).reshape(B, T, RW_D) * lp['rw_ln_g'] + lp['rw_ln_b']
    bonus = (jnp.sum(r * k * lp['rw_r_k'].astype(f32), -1, keepdims=True) * v).reshape(B, T, RW_D)
    return (yn + bonus) * g, wkv, p_rw[:, -1]


def _layer(x, pos, attend, wkv0, shift0, lam_init, lp):
    B, T, _ = x.shape
    f32 = jnp.float32
    p = x @ lp['w_in']
    q = _rope(p[..., O_Q:O_K].reshape(B, T, DA_HEADS, 2, DA_QK), pos)
    k = _rope(p[..., O_K:O_V].reshape(B, T, DA_HEADS, 2, DA_QK), pos)
    v = p[..., O_V:O_G].reshape(B, T, DA_HEADS, DA_V)
    gates = jax.nn.sigmoid(p[..., O_G:O_RW].astype(f32))
    lam = (jnp.exp(jnp.sum(lp['lam_q1'].astype(f32) * lp['lam_k1'].astype(f32)))
           - jnp.exp(jnp.sum(lp['lam_q2'].astype(f32) * lp['lam_k2'].astype(f32))) + lam_init)
    o_a = _da_subnorm(attend(q, k, v, lam), lp['da_norm_g'], lam_init)
    o_b, wkv, shift = _rwkv_branch(p[..., O_RW:], shift0, wkv0, lp)
    merged = (gates[..., :D_MODEL] * o_a + gates[..., D_MODEL:] * o_b).astype(x.dtype)
    h = _layer_norm(ALPHA * x + merged @ lp['w_out'], lp['ln1_g'], lp['ln1_b'])
    f = jnp.square(jax.nn.relu(h @ lp['w_up'])) @ lp['w_down']
    y = _layer_norm(ALPHA * h + f, lp['ln2_g'], lp['ln2_b'])
    return y, k.reshape(B, T, DA_HEADS, 2 * DA_QK), v, wkv, shift


def setup_inputs(seed: int = 0) -> dict:
    key = jax.random.key(seed)
    ks = jax.random.split(key, 40)
    f32 = jnp.float32
    n_pages = PAST_LEN // PAGE_SIZE
    n_used = DEC_BATCH * n_pages
    n_pool = n_used + max(1, n_used // 4)

    def nrm(k, shape, s):
        return jax.random.normal(k, shape, f32) * s

    col_scale = (jnp.ones((IN_W,), f32).at[O_V:O_G].set(BETA)
                 .at[O_RW + 2 * RW_D:O_RW + 3 * RW_D].set(BETA))
    page_table = jax.random.permutation(ks[7], n_pool)[:n_used].reshape(DEC_BATCH, n_pages).astype(jnp.int32)
    return {
        'x_prompt': nrm(ks[0], (BATCH, SEQ, D_MODEL), 1.0),
        'x_sample': nrm(ks[1], (DEC_BATCH, DEC_SEQ, D_MODEL), 1.0),
        'cache_k': nrm(ks[2], (DEPTH, n_pool, PAGE_SIZE, DA_HEADS, 2 * DA_QK), 1.0),
        'cache_v': nrm(ks[3], (DEPTH, n_pool, PAGE_SIZE, DA_HEADS, DA_V), 1.0),
        'state_wkv': nrm(ks[4], (DEPTH, DEC_BATCH, RW_HEADS, RW_HEAD, RW_HEAD), 0.3),
        'state_shift': nrm(ks[5], (DEPTH, DEC_BATCH, RW_W), 1.0),
        'page_table': page_table,
        'w_in': nrm(ks[8], (DEPTH, D_MODEL, IN_W), D_MODEL ** -0.5) * col_scale,
        'lam_q1': nrm(ks[9], (DEPTH, DA_QK), 0.1),
        'lam_k1': nrm(ks[10], (DEPTH, DA_QK), 0.1),
        'lam_q2': nrm(ks[11], (DEPTH, DA_QK), 0.1),
        'lam_k2': nrm(ks[12], (DEPTH, DA_QK), 0.1),
        'da_norm_g': 1.0 + nrm(ks[13], (DEPTH, DA_V_W), 0.02),
        'rw_mu': jax.random.uniform(ks[14], (DEPTH, RW_W), f32),
        'rw_w0': jax.random.uniform(ks[15], (DEPTH, RW_D), f32, minval=-6.0, maxval=1.0),
        'rw_w2': nrm(ks[16], (DEPTH, DECAY_LORA, RW_D), DECAY_LORA ** -0.5),
        'rw_a0': nrm(ks[17], (DEPTH, RW_D), 0.1),
        'rw_a2': nrm(ks[18], (DEPTH, AAA_LORA, RW_D), AAA_LORA ** -0.5),
        'rw_g2': nrm(ks[19], (DEPTH, GATE_LORA, RW_D), GATE_LORA ** -0.5),
        'rw_k_k': 0.85 + nrm(ks[20], (DEPTH, RW_D), 0.02),
        'rw_k_a': 1.0 + nrm(ks[21], (DEPTH, RW_D), 0.02),
        'rw_r_k': nrm(ks[22], (DEPTH, RW_HEADS, RW_HEAD), 0.1),
        'rw_ln_g': 1.0 + nrm(ks[23], (DEPTH, RW_D), 0.02),
        'rw_ln_b': nrm(ks[24], (DEPTH, RW_D), 0.02),
        'w_out': nrm(ks[25], (DEPTH, D_MODEL, D_MODEL), BETA * D_MODEL ** -0.5),
        'ln1_g': 1.0 + nrm(ks[26], (DEPTH, D_MODEL), 0.02),
        'ln1_b': nrm(ks[27], (DEPTH, D_MODEL), 0.02),
        'w_up': nrm(ks[28], (DEPTH, D_MODEL, D_FF), BETA * D_MODEL ** -0.5),
        'w_down': nrm(ks[29], (DEPTH, D_FF, D_MODEL), BETA * D_FF ** -0.5),
        'ln2_g': 1.0 + nrm(ks[30], (DEPTH, D_MODEL), 0.02),
        'ln2_b': nrm(ks[31], (DEPTH, D_MODEL), 0.02),
    }


def reference(x_prompt, x_sample, cache_k, cache_v, state_wkv, state_shift, page_table,
              w_in, lam_q1, lam_k1, lam_q2, lam_k2, da_norm_g,
              rw_mu, rw_w0, rw_w2, rw_a0, rw_a2, rw_g2, rw_k_k, rw_k_a, rw_r_k, rw_ln_g, rw_ln_b,
              w_out, ln1_g, ln1_b, w_up, w_down, ln2_g, ln2_b):
    B, T, _ = x_prompt.shape
    Bd, Td, _ = x_sample.shape
    past_len = page_table.shape[1] * PAGE_SIZE
    pos_p = jnp.arange(T, dtype=jnp.int32)
    pos_s = past_len + jnp.arange(Td, dtype=jnp.int32)
    yp, ys = x_prompt, x_sample
    kp_l, vp_l, ks_l, vs_l, wp_l, ws_l, hp_l, hs_l = [], [], [], [], [], [], [], []
    for l in range(DEPTH):
        lam_init = 0.8 - 0.6 * math.exp(-0.3 * l)
        lp = {
            'w_in': w_in[l], 'lam_q1': lam_q1[l], 'lam_k1': lam_k1[l], 'lam_q2': lam_q2[l],
            'lam_k2': lam_k2[l], 'da_norm_g': da_norm_g[l], 'rw_mu': rw_mu[l], 'rw_w0': rw_w0[l],
            'rw_w2': rw_w2[l], 'rw_a0': rw_a0[l], 'rw_a2': rw_a2[l], 'rw_g2': rw_g2[l],
            'rw_k_k': rw_k_k[l], 'rw_k_a': rw_k_a[l], 'rw_r_k': rw_r_k[l], 'rw_ln_g': rw_ln_g[l],
            'rw_ln_b': rw_ln_b[l], 'w_out': w_out[l], 'ln1_g': ln1_g[l], 'ln1_b': ln1_b[l],
            'w_up': w_up[l], 'w_down': w_down[l], 'ln2_g': ln2_g[l], 'ln2_b': ln2_b[l],
        }
        wkv0_p = jnp.zeros((B, RW_HEADS, RW_HEAD, RW_HEAD), jnp.float32)
        shift0_p = jnp.zeros((B, RW_W), yp.dtype)
        yp, kp, vp, wp, hp = _layer(yp, pos_p, _diff_attn_prompt, wkv0_p, shift0_p, lam_init, lp)
        attend_s = functools.partial(_diff_attn_decode, cache_k=cache_k[l], cache_v=cache_v[l],
                                     page_table=page_table)
        ys, ks, vs, ws, hs = _layer(ys, pos_s, attend_s, state_wkv[l], state_shift[l], lam_init, lp)
        kp_l.append(kp.reshape(B * T // PAGE_SIZE, PAGE_SIZE, DA_HEADS, 2 * DA_QK).astype(cache_k.dtype))
        vp_l.append(vp.reshape(B * T // PAGE_SIZE, PAGE_SIZE, DA_HEADS, DA_V).astype(cache_v.dtype))
        ks_l.append(ks.astype(cache_k.dtype))
        vs_l.append(vs.astype(cache_v.dtype))
        wp_l.append(wp.astype(state_wkv.dtype))
        ws_l.append(ws.astype(state_wkv.dtype))
        hp_l.append(hp.astype(state_shift.dtype))
        hs_l.append(hs.astype(state_shift.dtype))
    return (yp, ys, jnp.stack(kp_l), jnp.stack(vp_l), jnp.stack(ks_l), jnp.stack(vs_l),
            jnp.stack(wp_l), jnp.stack(ws_l), jnp.stack(hp_l), jnp.stack(hs_l))
```

```python
import functools
import math

import numpy as np
import jax
import jax.numpy as jnp
from jax import lax
from jax.experimental import pallas as pl
from jax.experimental.pallas import tpu as pltpu

F32 = jnp.float32
BF16 = jnp.bfloat16

DA_QK = 64
ROPE_THETA = 10000.0
RW_HEAD = 64
DECAY_LORA = 64
AAA_LORA = 64
GATE_LORA = 160
LN_EPS = 1e-5
SUBLN_EPS = 1e-5
GN_EPS = 64e-5

LANES = 128
LORA_PAD = (LANES, LANES, 2 * LANES)


def _tile(m, prefs):
    for t in prefs:
        if m % t == 0:
            return t
    raise ValueError(f"no tile for {m} in {prefs}")


def _layer_norm(z, g, b):
    mu = jnp.mean(z, axis=-1, keepdims=True)
    zc = z - mu
    var = jnp.mean(zc * zc, axis=-1, keepdims=True)
    return zc * lax.rsqrt(var + LN_EPS) * g + b


def _sigmoid(z):
    return 1.0 / (1.0 + jnp.exp(-z))


def _proj_body(x_ref, w_ref, o_ref):
    o_ref[...] = jnp.dot(x_ref[...].astype(BF16), w_ref[...], preferred_element_type=F32)


def _proj(x, w, tn=512):
    m, d = x.shape
    n = w.shape[1]
    tm = _tile(m, (1024, 512, 256, 128))
    return pl.pallas_call(
        _proj_body,
        grid=(m // tm, n // tn),
        in_specs=[pl.BlockSpec((tm, d), lambda i, j: (i, 0)), pl.BlockSpec((d, tn), lambda i, j: (0, j))],
        out_specs=pl.BlockSpec((tm, tn), lambda i, j: (i, j)),
        out_shape=jax.ShapeDtypeStruct((m, n), F32),
        compiler_params=pltpu.CompilerParams(dimension_semantics=("parallel", "arbitrary")),
        name="proj",
    )(x, w)


def _rope_body(q_ref, k_ref, v_ref, cos_ref, sin_ref, qo_ref, ko_ref, kbo_ref, vo_ref, vbo_ref, *, d, q_scale):
    c = cos_ref[...]
    s = sin_ref[...]
    lane = lax.broadcasted_iota(jnp.int32, (1, LANES), 1)
    first = (lane % DA_QK) < (DA_QK // 2)
    for ci in range(d // LANES):
        sl = slice(ci * LANES, (ci + 1) * LANES)
        for src, scale, dsts in ((q_ref, q_scale, (None, qo_ref)), (k_ref, 1.0, (ko_ref, kbo_ref))):
            xc = src[:, sl]
            partner = jnp.where(first, pltpu.roll(xc, LANES - DA_QK // 2, 1), pltpu.roll(xc, DA_QK // 2, 1))
            y = xc * c + partner * s
            if dsts[0] is not None:
                dsts[0][:, sl] = y
            dsts[1][:, sl] = (y * scale).astype(BF16)
    v = v_ref[...]
    vo_ref[...] = v
    vbo_ref[...] = v.astype(BF16)


def _rope_tables(pos):
    half = DA_QK // 2
    inv = ROPE_THETA ** (-np.arange(half, dtype=np.float64) / half)
    ang = np.asarray(pos, np.float64)[:, None] * inv[None, :]
    cos = np.cos(ang)
    sin = np.sin(ang)
    cos64 = np.concatenate([cos, cos], -1)
    sin64 = np.concatenate([-sin, sin], -1)
    reps = LANES // DA_QK
    return (jnp.asarray(np.tile(cos64, (1, reps)), F32), jnp.asarray(np.tile(sin64, (1, reps)), F32))


def _rope(pa, d, t_len, pos0):
    m = pa.shape[0]
    if t_len == 1:
        tr = _tile(m, (256, 128, 64, 32, 16, 8))
        cos, sin = _rope_tables([pos0])
        tab_spec = pl.BlockSpec((1, LANES), lambda i: (0, 0))
    else:
        tr = _tile(t_len, (256, 128, 64, 32, 16, 8))
        cos, sin = _rope_tables(pos0 + np.arange(t_len))
        nt = t_len // tr
        tab_spec = pl.BlockSpec((tr, LANES), lambda i: (i % nt, 0))
    col = lambda c: pl.BlockSpec((tr, d), lambda i: (i, c))
    out = pl.BlockSpec((tr, d), lambda i: (i, 0))
    return pl.pallas_call(
        functools.partial(_rope_body, d=d, q_scale=DA_QK ** -0.5),
        grid=(m // tr,),
        in_specs=[col(0), col(1), col(2), tab_spec, tab_spec],
        out_specs=[out] * 5,
        out_shape=[
            jax.ShapeDtypeStruct((m, d), BF16),
            jax.ShapeDtypeStruct((m, d), F32),
            jax.ShapeDtypeStruct((m, d), BF16),
            jax.ShapeDtypeStruct((m, d), F32),
            jax.ShapeDtypeStruct((m, d), BF16),
        ],
        compiler_params=pltpu.CompilerParams(dimension_semantics=("parallel",)),
        name="rope",
    )(pa, pa, pa, cos, sin)


def _lam_value(lam_ref, lam_init):
    lv = lam_ref[...]
    s1 = jnp.sum(lv[0:1] * lv[1:2], axis=-1, keepdims=True)
    s2 = jnp.sum(lv[2:3] * lv[3:4], axis=-1, keepdims=True)
    return jnp.exp(s1) - jnp.exp(s2) + lam_init


def _subnorm(o, g, lam_init):
    o = o * lax.rsqrt(jnp.mean(o * o, axis=-1, keepdims=True) + SUBLN_EPS)
    return o * g * (1.0 - lam_init)


def _attn_body(lam_ref, q_ref, k_ref, v_ref, g_ref, o_ref, *, tq, lam_init):
    i = pl.program_id(2)
    q = q_ref[...]
    lane = lax.broadcasted_iota(jnp.int32, (tq, LANES), 1)
    zero = jnp.zeros_like(q)
    q2 = jnp.concatenate([jnp.where(lane < DA_QK, q, zero), jnp.where(lane >= DA_QK, q, zero)], axis=0)

    def scores(j):
        kb = k_ref[pl.ds(j * tq, tq), :]
        return lax.dot_general(q2, kb, (((1,), (1,)), ((), ())), preferred_element_type=F32)

    s = scores(i)
    row = lax.broadcasted_iota(jnp.int32, (2 * tq, tq), 0) % tq
    colk = lax.broadcasted_iota(jnp.int32, (2 * tq, tq), 1)
    s = jnp.where(colk <= row, s, -jnp.inf)
    m0 = jnp.max(s, axis=-1, keepdims=True)
    p = jnp.exp(s - m0)
    l0 = jnp.sum(p, axis=-1, keepdims=True)
    acc0 = jnp.dot(p.astype(BF16), v_ref[pl.ds(i * tq, tq), :], preferred_element_type=F32)

    def body(j, carry):
        m, l, acc = carry
        s = scores(j)
        m_new = jnp.maximum(m, jnp.max(s, axis=-1, keepdims=True))
        alpha = jnp.exp(m - m_new)
        p = jnp.exp(s - m_new)
        l = alpha * l + jnp.sum(p, axis=-1, keepdims=True)
        acc = alpha * acc + jnp.dot(p.astype(BF16), v_ref[pl.ds(j * tq, tq), :], preferred_element_type=F32)
        return m_new, l, acc

    _, l, acc = lax.fori_loop(0, i, body, (m0, l0, acc0))
    o = acc / l
    lam = _lam_value(lam_ref, lam_init)
    o = o[:tq] - lam * o[tq:]
    o_ref[...] = _subnorm(o, g_ref[...], lam_init)


def _attn_prompt(lamv, q_bf, k_bf, v_bf, g, b, t_len, heads, lam_init):
    m, d = q_bf.shape
    tq = _tile(t_len, (256, 128))
    nq = t_len // tq
    return pl.pallas_call(
        functools.partial(_attn_body, tq=tq, lam_init=lam_init),
        grid=(b, heads, nq),
        in_specs=[
            pl.BlockSpec((8, LANES), lambda bi, h, i: (0, 0)),
            pl.BlockSpec((tq, LANES), lambda bi, h, i: (bi * nq + i, h)),
            pl.BlockSpec((t_len, LANES), lambda bi, h, i: (bi, h)),
            pl.BlockSpec((t_len, LANES), lambda bi, h, i: (bi, h)),
            pl.BlockSpec((1, LANES), lambda bi, h, i: (0, h)),
        ],
        out_specs=pl.BlockSpec((tq, LANES), lambda bi, h, i: (bi * nq + i, h)),
        out_shape=jax.ShapeDtypeStruct((m, d), F32),
        compiler_params=pltpu.CompilerParams(dimension_semantics=("parallel", "parallel", "arbitrary")),
        name="attn_prompt",
    )(lamv, q_bf, k_bf, v_bf, g)


def _dec_body(pt_ref, lam_ref, qz_ref, kn_ref, vn_ref, g_ref, *refs, pp, heads, lam_init):
    k_refs = refs[:pp]
    v_refs = refs[pp:2 * pp]
    o_ref = refs[2 * pp]
    m_sc, l_sc, acc_sc = refs[2 * pp + 1:]
    j = pl.program_id(1)
    shape = (heads, 8, LANES)

    @pl.when(j == 0)
    def _():
        s = jnp.sum(qz_ref[0] * kn_ref[0], axis=-1, keepdims=True)
        m_sc[...] = jnp.broadcast_to(s, shape)
        l_sc[...] = jnp.ones(shape, F32)
        acc_sc[...] = vn_ref[0]

    for pi in range(pp):
        for h in range(heads):
            kh = k_refs[pi][0, 0, :, h, :].astype(BF16)
            vh = v_refs[pi][0, 0, :, h, :].astype(BF16)
            s = lax.dot_general(qz_ref[0, h].astype(BF16), kh, (((1,), (1,)), ((), ())), preferred_element_type=F32)
            m_old = m_sc[h]
            m_new = jnp.maximum(m_old, jnp.max(s, axis=-1, keepdims=True))
            corr = jnp.exp(m_old - m_new)
            p = jnp.exp(s - m_new)
            l_sc[h] = l_sc[h] * corr + jnp.sum(p, axis=-1, keepdims=True)
            acc_sc[h] = acc_sc[h] * corr + jnp.dot(p.astype(BF16), vh, preferred_element_type=F32)
            m_sc[h] = m_new

    @pl.when(j == pl.num_programs(1) - 1)
    def _():
        lam = _lam_value(lam_ref, lam_init)
        for h in range(heads):
            o = acc_sc[h] / l_sc[h]
            o = o[0:1] - lam * o[1:2]
            o_ref[0, h:h + 1, :] = _subnorm(o, g_ref[h:h + 1, :], lam_init)


def _attn_decode(layer, lamv, qz, k_new, v_new, g, cache_k, cache_v, page_table, lam_init):
    bd, heads = qz.shape[:2]
    n_pages = page_table.shape[1]
    page = cache_k.shape[2]
    pp = _tile(n_pages, (4, 2, 1))

    def cache_spec(pi):
        return pl.BlockSpec((1, 1, page, heads, LANES), lambda b, j, pt: (layer, pt[b, j * pp + pi], 0, 0, 0))

    grid_spec = pltpu.PrefetchScalarGridSpec(
        num_scalar_prefetch=1,
        grid=(bd, n_pages // pp),
        in_specs=[
            pl.BlockSpec((8, LANES), lambda b, j, pt: (0, 0)),
            pl.BlockSpec((1, heads, 8, LANES), lambda b, j, pt: (b, 0, 0, 0)),
            pl.BlockSpec((1, heads, 8, LANES), lambda b, j, pt: (b, 0, 0, 0)),
            pl.BlockSpec((1, heads, 8, LANES), lambda b, j, pt: (b, 0, 0, 0)),
            pl.BlockSpec((heads, LANES), lambda b, j, pt: (0, 0)),
        ] + [cache_spec(pi) for pi in range(pp)] * 1 + [cache_spec(pi) for pi in range(pp)],
        out_specs=pl.BlockSpec((1, heads, LANES), lambda b, j, pt: (b, 0, 0)),
        scratch_shapes=[pltpu.VMEM((heads, 8, LANES), F32)] * 3,
    )
    return pl.pallas_call(
        functools.partial(_dec_body, pp=pp, heads=heads, lam_init=lam_init),
        grid_spec=grid_spec,
        out_shape=jax.ShapeDtypeStruct((bd, heads, LANES), F32),
        compiler_params=pltpu.CompilerParams(dimension_semantics=("parallel", "arbitrary")),
        name="attn_decode",
    )(page_table, lamv, qz, k_new, v_new, g, *([cache_k] * pp), *([cache_v] * pp))


def _prep_body(p_ref, prev_ref, mu_ref, w0_ref, w2_ref, a0_ref, a2_ref, g2_ref, kk_ref, ka_ref,
               r_o, w_o, k_o, kk_o, a_o, v_o, g_o, *, d, tr, blocks_per_seq, seq):
    p = p_ref[...]
    if seq:
        i = pl.program_id(0)
        rolled = pltpu.roll(p, 1, 0)
        prev_row = prev_ref[7:8, :]
        prev_row = jnp.where(i % blocks_per_seq == 0, jnp.zeros_like(prev_row), prev_row)
        row = lax.broadcasted_iota(jnp.int32, (tr, 1), 0)
        prev = jnp.where(row == 0, prev_row, rolled)
    else:
        prev = prev_ref[...]
    xm = p + (prev - p) * mu_ref[...]
    r = xm[:, 0:d]
    k = xm[:, d:2 * d]
    v = xm[:, 2 * d:3 * d]
    o = 3 * d
    wd = xm[:, o:o + LORA_PAD[0]]
    ad = xm[:, o + LORA_PAD[0]:o + LORA_PAD[0] + LORA_PAD[1]]
    gd = xm[:, o + LORA_PAD[0] + LORA_PAD[1]:]
    u = w0_ref[...] + jnp.dot(jnp.tanh(wd).astype(BF16), w2_ref[...], preferred_element_type=F32)
    w_o[...] = jnp.exp(-math.exp(-0.5) * _sigmoid(u))
    a = _sigmoid(a0_ref[...] + jnp.dot(ad.astype(BF16), a2_ref[...], preferred_element_type=F32))
    g_o[...] = jnp.dot(_sigmoid(gd).astype(BF16), g2_ref[...], preferred_element_type=F32)
    r_o[...] = r
    v_o[...] = v
    a_o[...] = a
    kk_o[...] = k * kk_ref[...]
    k_o[...] = k * (1.0 + (a - 1.0) * ka_ref[...])


def _rwkv_prep(p_rw, prev, d, t_len, mu, w0, w2, a0, a2, g2, k_k, k_a):
    m, wpad = p_rw.shape
    seq = t_len > 1
    tr = _tile(t_len if seq else m, (256, 128, 64, 32, 16, 8))
    if seq:
        prev_arr = p_rw
        prev_spec = pl.BlockSpec((8, wpad), lambda i: (jnp.maximum(i * (tr // 8) - 1, 0), 0))
    else:
        prev_arr = prev
        prev_spec = pl.BlockSpec((tr, wpad), lambda i: (i, 0))
    full = lambda a: pl.BlockSpec(a.shape, lambda i: (0, 0))
    out = pl.BlockSpec((tr, d), lambda i: (i, 0))
    return pl.pallas_call(
        functools.partial(_prep_body, d=d, tr=tr, blocks_per_seq=max(t_len // tr, 1), seq=seq),
        grid=(m // tr,),
        in_specs=[pl.BlockSpec((tr, wpad), lambda i: (i, 0)), prev_spec, full(mu), full(w0), full(w2), full(a0),
                  full(a2), full(g2), full(k_k), full(k_a)],
        out_specs=[out] * 7,
        out_shape=[jax.ShapeDtypeStruct((m, d), F32)] * 7,
        compiler_params=pltpu.CompilerParams(dimension_semantics=("arbitrary",)),
        name="rwkv_prep",
    )(p_rw, prev_arr, mu, w0, w2, a0, a2, g2, k_k, k_a)


def _wkv_body(r_ref, w_ref, k_ref, kk_ref, a_ref, v_ref, s0_ref, rk_ref, lng_ref, lnb_ref,
              y_ref, st_ref, s_sc, y_sc, *, tb, n):
    @pl.when(pl.program_id(1) == 0)
    def _():
        s_sc[...] = s0_ref[...]

    def step(t, carry):
        r = r_ref[t]
        w = w_ref[t]
        k2 = k_ref[t]
        kkr = kk_ref[t]
        v = v_ref[t]
        norm = jnp.sqrt(jnp.sum(kkr * kkr, axis=0, keepdims=True))
        kk = kkr / jnp.maximum(norm, 1e-12)
        ka = kk * a_ref[t]
        nkk = -kk
        for vi in range(n):
            sv = s_sc[vi]
            sa = jnp.sum(sv * nkk, axis=0, keepdims=True)
            sv = sv * w + sa * ka + v[vi:vi + 1, :] * k2
            s_sc[vi] = sv
            y_sc[vi:vi + 1, :] = jnp.sum(sv * r, axis=0, keepdims=True)
        y = y_sc[...]
        mu = jnp.mean(y, axis=0, keepdims=True)
        yc = y - mu
        var = jnp.mean(yc * yc, axis=0, keepdims=True)
        bonus = jnp.sum(r * k2 * rk_ref[...], axis=0, keepdims=True) * v
        y_ref[t] = yc * lax.rsqrt(var + GN_EPS) * lng_ref[...] + lnb_ref[...] + bonus
        return carry

    lax.fori_loop(0, tb, step, 0)

    @pl.when(pl.program_id(1) == pl.num_programs(1) - 1)
    def _():
        st_ref[...] = s_sc[...]


def _wkv(r, w, k2, kk, a, v, s0, rk, lng, lnb):
    t_len, n, c = r.shape
    tb = _tile(t_len, (16, 8, 4, 2, 1))
    seqs = pl.BlockSpec((tb, n, LANES), lambda ci, ti: (ti, 0, ci))
    state = pl.BlockSpec((n, n, LANES), lambda ci, ti: (0, 0, ci))
    par = pl.BlockSpec((n, LANES), lambda ci, ti: (0, 0))
    return pl.pallas_call(
        functools.partial(_wkv_body, tb=tb, n=n),
        grid=(c // LANES, t_len // tb),
        in_specs=[seqs] * 6 + [state, par, par, par],
        out_specs=[seqs, state],
        out_shape=[jax.ShapeDtypeStruct((t_len, n, c), F32), jax.ShapeDtypeStruct((n, n, c), F32)],
        scratch_shapes=[pltpu.VMEM((n, n, LANES), F32), pltpu.VMEM((n, LANES), F32)],
        compiler_params=pltpu.CompilerParams(dimension_semantics=("parallel", "arbitrary")),
        name="wkv",
    )(r, w, k2, kk, a, v, s0, rk, lng, lnb)


def _wout_body(ga_ref, gb_ref, oa_ref, yb_ref, g_ref, x_ref, w_ref, lg_ref, lb_ref, h_ref, *, alpha):
    ob = yb_ref[...] * g_ref[...]
    merged = _sigmoid(ga_ref[...]) * oa_ref[...] + _sigmoid(gb_ref[...]) * ob
    acc = jnp.dot(merged.astype(BF16), w_ref[...], preferred_element_type=F32)
    h_ref[...] = _layer_norm(alpha * x_ref[...] + acc, lg_ref[...], lb_ref[...])


def _merge_wout(pa, o_a, y_b, g, x, w_out, ln_g, ln_b, alpha):
    m, d = x.shape
    tm = _tile(m, (256, 128))
    row = pl.BlockSpec((tm, d), lambda i: (i, 0))
    vec = pl.BlockSpec((1, d), lambda i: (0, 0))
    return pl.pallas_call(
        functools.partial(_wout_body, alpha=alpha),
        grid=(m // tm,),
        in_specs=[pl.BlockSpec((tm, d), lambda i: (i, 3)), pl.BlockSpec((tm, d), lambda i: (i, 4)), row, row, row, row,
                  pl.BlockSpec((d, d), lambda i: (0, 0)), vec, vec],
        out_specs=row,
        out_shape=jax.ShapeDtypeStruct((m, d), F32),
        compiler_params=pltpu.CompilerParams(dimension_semantics=("parallel",)),
        name="merge_wout",
    )(pa, pa, o_a, y_b, g, x, w_out, ln_g, ln_b)


def _ffn_body(h_ref, wu_ref, wd_ref, lg_ref, lb_ref, y_ref, hb_sc, acc_sc, *, alpha):
    j = pl.program_id(1)

    @pl.when(j == 0)
    def _():
        hb_sc[...] = h_ref[...].astype(BF16)
        acc_sc[...] = jnp.zeros_like(acc_sc)

    u = jnp.dot(hb_sc[...], wu_ref[...], preferred_element_type=F32)
    u = jnp.square(jnp.maximum(u, 0.0)).astype(BF16)
    acc_sc[...] += jnp.dot(u, wd_ref[...], preferred_element_type=F32)

    @pl.when(j == pl.num_programs(1) - 1)
    def _():
        y_ref[...] = _layer_norm(alpha * h_ref[...] + acc_sc[...], lg_ref[...], lb_ref[...])


def _ffn(h, w_up, w_down, ln_g, ln_b, alpha, tf=512):
    m, d = h.shape
    ff = w_up.shape[1]
    tm = _tile(m, (1024, 512, 256, 128))
    vec = pl.BlockSpec((1, d), lambda i, j: (0, 0))
    return pl.pallas_call(
        functools.partial(_ffn_body, alpha=alpha),
        grid=(m // tm, ff // tf),
        in_specs=[pl.BlockSpec((tm, d), lambda i, j: (i, 0)), pl.BlockSpec((d, tf), lambda i, j: (0, j)),
                  pl.BlockSpec((tf, d), lambda i, j: (j, 0)), vec, vec],
        out_specs=pl.BlockSpec((tm, d), lambda i, j: (i, 0)),
        out_shape=jax.ShapeDtypeStruct((m, d), F32),
        scratch_shapes=[pltpu.VMEM((tm, d), BF16), pltpu.VMEM((tm, d), F32)],
        compiler_params=pltpu.CompilerParams(dimension_semantics=("parallel", "arbitrary")),
        name="ffn",
    )(h, w_up, w_down, ln_g, ln_b)


def _rw_pad(a, d):
    o = 3 * d
    parts = [a[..., :o]]
    for width, padded in zip((DECAY_LORA, AAA_LORA, GATE_LORA), LORA_PAD):
        seg = a[..., o:o + width]
        parts.append(jnp.pad(seg, [(0, 0)] * (a.ndim - 1) + [(0, padded - width)]))
        o += width
    return jnp.concatenate(parts, axis=-1)


def _rw_unpad(a, d):
    o = 3 * d
    parts = [a[..., :o]]
    for width, padded in zip((DECAY_LORA, AAA_LORA, GATE_LORA), LORA_PAD):
        parts.append(a[..., o:o + width])
        o += padded
    return jnp.concatenate(parts, axis=-1)


def _pad_rows(a, rows):
    return jnp.pad(a, ((0, rows - a.shape[0]), (0, 0)))


def _to_chain(a, b, t_len, heads):
    n = a.shape[1] // heads
    return a.reshape(b, t_len, heads, n).transpose(1, 3, 0, 2).reshape(t_len, n, b * heads)


def _from_chain(a, b, heads):
    t_len, n, _ = a.shape
    return a.reshape(t_len, n, b, heads).transpose(2, 0, 3, 1).reshape(b * t_len, heads * n)


def _layer(layer, x, b, t_len, pos0, attend, wkv0, shift0, lam_init, alpha, lp):
    m, d = x.shape
    rw_heads = d // RW_HEAD
    pa = _proj(x, lp['w_a'])
    p_rw = _proj(x, lp['w_b'])
    q_bf, k_rot, k_bf, v, v_bf = _rope(pa, d, t_len, pos0)
    o_a = attend(q_bf, k_rot, k_bf, v, v_bf)

    r, w, k2, kk, a, vr, g = _rwkv_prep(p_rw, shift0, d, t_len, lp['rw_mu'], lp['rw_w0'], lp['rw_w2'], lp['rw_a0'],
                                        lp['rw_a2'], lp['rw_g2'], lp['rw_k_k'], lp['rw_k_a'])
    chain = lambda z: _to_chain(z, b, t_len, rw_heads)
    s0 = wkv0.transpose(2, 3, 0, 1).reshape(RW_HEAD, RW_HEAD, b * rw_heads)
    y_c, s_t = _wkv(chain(r), chain(w), chain(k2), chain(kk), chain(a), chain(vr), s0,
                    lp['rk_c'], lp['lng_c'], lp['lnb_c'])
    y_b = _from_chain(y_c, b, rw_heads)
    wkv = s_t.reshape(RW_HEAD, RW_HEAD, b, rw_heads).transpose(2, 3, 0, 1)
    shift = _rw_unpad(p_rw.reshape(b, t_len, -1)[:, -1], d)

    h = _merge_wout(pa, o_a, y_b, g, x, lp['w_out'], lp['ln1_g'], lp['ln1_b'], alpha)
    y = _ffn(h, lp['w_up'], lp['w_down'], lp['ln2_g'], lp['ln2_b'], alpha)
    return y, k_rot, v, wkv, shift


def kernel(x_prompt, x_sample, cache_k, cache_v, state_wkv, state_shift, page_table, w_in, lam_q1, lam_k1, lam_q2, lam_k2, da_norm_g, rw_mu, rw_w0, rw_w2, rw_a0, rw_a2, rw_g2, rw_k_k, rw_k_a, rw_r_k, rw_ln_g, rw_ln_b, w_out, ln1_g, ln1_b, w_up, w_down, ln2_g, ln2_b):
    b, t_len, d = x_prompt.shape
    bd, td, _ = x_sample.shape
    assert td == 1, "decode path handles one new token per sample"
    depth = w_in.shape[0]
    page = cache_k.shape[2]
    heads = cache_k.shape[3]
    assert cache_k.shape[4] == LANES and cache_v.shape[4] == LANES and heads * LANES == d
    assert t_len % page == 0
    past_len = page_table.shape[1] * page
    rw_heads = d // RW_HEAD
    alpha = (2 * depth) ** 0.25
    o_rw = 5 * d

    yp = x_prompt.reshape(b * t_len, d)
    ys = x_sample.reshape(bd, d)
    outs = [[] for _ in range(8)]
    for l in range(depth):
        lam_init = 0.8 - 0.6 * math.exp(-0.3 * l)
        row = lambda a: a[l].reshape(1, -1)
        lamv = _pad_rows(jnp.pad(jnp.stack([lam_q1[l], lam_k1[l], lam_q2[l], lam_k2[l]]), ((0, 0), (0, LANES - DA_QK))), 8)
        lane_heads = lambda a: jnp.tile(a.reshape(rw_heads, RW_HEAD).T, (1, LANES // rw_heads))
        lp = {
            'w_a': w_in[l][:, :o_rw].astype(BF16),
            'w_b': _rw_pad(w_in[l][:, o_rw:], d).astype(BF16),
            'rw_mu': _rw_pad(row(rw_mu), d),
            'rw_w0': row(rw_w0), 'rw_a0': row(rw_a0), 'rw_k_k': row(rw_k_k), 'rw_k_a': row(rw_k_a),
            'rw_w2': _pad_rows(rw_w2[l], LORA_PAD[0]).astype(BF16),
            'rw_a2': _pad_rows(rw_a2[l], LORA_PAD[1]).astype(BF16),
            'rw_g2': _pad_rows(rw_g2[l], LORA_PAD[2]).astype(BF16),
            'rk_c': lane_heads(rw_r_k[l]), 'lng_c': lane_heads(rw_ln_g[l]), 'lnb_c': lane_heads(rw_ln_b[l]),
            'w_out': w_out[l].astype(BF16), 'ln1_g': row(ln1_g), 'ln1_b': row(ln1_b),
            'w_up': w_up[l].astype(BF16), 'w_down': w_down[l].astype(BF16), 'ln2_g': row(ln2_g), 'ln2_b': row(ln2_b),
        }
        g_da = row(da_norm_g)

        def attend_p(q_bf, k_rot, k_bf, v, v_bf):
            return _attn_prompt(lamv, q_bf, k_bf, v_bf, g_da, b, t_len, heads, lam_init)

        def attend_s(q_bf, k_rot, k_bf, v, v_bf):
            q = q_bf.astype(F32).reshape(bd, heads, 1, LANES)
            lane = jnp.arange(LANES)
            qz = jnp.concatenate([jnp.where(lane < DA_QK, q, 0.0), jnp.where(lane >= DA_QK, q, 0.0),
                                  jnp.zeros((bd, heads, 6, LANES), F32)], axis=2)
            rows8 = lambda z: jnp.broadcast_to(z.reshape(bd, heads, 1, LANES), (bd, heads, 8, LANES))
            o = _attn_decode(l, lamv, qz, rows8(k_rot), rows8(v),
                             g_da.reshape(heads, LANES), cache_k, cache_v, page_table, lam_init)
            return o.reshape(bd, d)

        wkv0_p = jnp.zeros((b, rw_heads, RW_HEAD, RW_HEAD), F32)
        yp, kp, vp, wp, hp = _layer(l, yp, b, t_len, 0, attend_p, wkv0_p, None, lam_init, alpha, lp)
        ys, ks, vs, ws, hs = _layer(l, ys, bd, 1, past_len, attend_s, state_wkv[l], _rw_pad(state_shift[l], d),
                                    lam_init, alpha, lp)
        outs[0].append(kp.reshape(b * t_len // page, page, heads, LANES))
        outs[1].append(vp.reshape(b * t_len // page, page, heads, LANES))
        outs[2].append(ks.reshape(bd, 1, heads, LANES))
        outs[3].append(vs.reshape(bd, 1, heads, LANES))
        outs[4].append(wp)
        outs[5].append(ws)
        outs[6].append(hp)
        outs[7].append(hs)
    return (yp.reshape(b, t_len, d), ys.reshape(bd, 1, d)) + tuple(jnp.stack(o) for o in outs)
```

```python
import functools
import math

import numpy as np
import jax
import jax.numpy as jnp
from jax import lax
from jax.experimental import pallas as pl
from jax.experimental.pallas import tpu as pltpu

F32 = jnp.float32
BF16 = jnp.bfloat16

DA_QK = 64
ROPE_THETA = 10000.0
RW_HEAD = 64
DECAY_LORA = 64
AAA_LORA = 64
GATE_LORA = 160
LN_EPS = 1e-5
SUBLN_EPS = 1e-5
GN_EPS = 64e-5

LANES = 128
LORA_PAD = (LANES, LANES, 2 * LANES)


def _tile(m, prefs):
    for t in prefs:
        if m % t == 0:
            return t
    raise ValueError(f"no tile for {m} in {prefs}")


def _layer_norm(z, g, b):
    mu = jnp.mean(z, axis=-1, keepdims=True)
    zc = z - mu
    var = jnp.mean(zc * zc, axis=-1, keepdims=True)
    return zc * lax.rsqrt(var + LN_EPS) * g + b


def _sigmoid(z):
    return 1.0 / (1.0 + jnp.exp(-z))


def _proj_body(x_ref, w_ref, o_ref):
    o_ref[...] = jnp.dot(x_ref[...].astype(BF16), w_ref[...], preferred_element_type=F32)


def _proj(x, w, tn=512):
    m, d = x.shape
    n = w.shape[1]
    tm = _tile(m, (1024, 512, 256, 128))
    return pl.pallas_call(
        _proj_body,
        grid=(m // tm, n // tn),
        in_specs=[pl.BlockSpec((tm, d), lambda i, j: (i, 0)), pl.BlockSpec((d, tn), lambda i, j: (0, j))],
        out_specs=pl.BlockSpec((tm, tn), lambda i, j: (i, j)),
        out_shape=jax.ShapeDtypeStruct((m, n), F32),
        compiler_params=pltpu.CompilerParams(dimension_semantics=("parallel", "arbitrary")),
        name="proj",
    )(x, w)


def _rope_body(q_ref, k_ref, v_ref, cos_ref, sin_ref, qo_ref, ko_ref, kbo_ref, vo_ref, vbo_ref, *, d, q_scale):
    c = cos_ref[...]
    s = sin_ref[...]
    lane = lax.broadcasted_iota(jnp.int32, (1, LANES), 1)
    first = (lane % DA_QK) < (DA_QK // 2)
    for ci in range(d // LANES):
        sl = slice(ci * LANES, (ci + 1) * LANES)
        for src, scale, dsts in ((q_ref, q_scale, (None, qo_ref)), (k_ref, 1.0, (ko_ref, kbo_ref))):
            xc = src[:, sl]
            partner = jnp.where(first, pltpu.roll(xc, LANES - DA_QK // 2, 1), pltpu.roll(xc, DA_QK // 2, 1))
            y = xc * c + partner * s
            if dsts[0] is not None:
                dsts[0][:, sl] = y
            dsts[1][:, sl] = (y * scale).astype(BF16)
    v = v_ref[...]
    vo_ref[...] = v
    vbo_ref[...] = v.astype(BF16)


def _rope_tables(pos):
    half = DA_QK // 2
    inv = ROPE_THETA ** (-np.arange(half, dtype=np.float64) / half)
    ang = np.asarray(pos, np.float64)[:, None] * inv[None, :]
    cos = np.cos(ang)
    sin = np.sin(ang)
    cos64 = np.concatenate([cos, cos], -1)
    sin64 = np.concatenate([-sin, sin], -1)
    reps = LANES // DA_QK
    return (jnp.asarray(np.tile(cos64, (1, reps)), F32), jnp.asarray(np.tile(sin64, (1, reps)), F32))


def _rope(pa, d, t_len, pos0):
    m = pa.shape[0]
    if t_len == 1:
        tr = _tile(m, (256, 128, 64, 32, 16, 8))
        cos, sin = _rope_tables([pos0])
        tab_spec = pl.BlockSpec((1, LANES), lambda i: (0, 0))
    else:
        tr = _tile(t_len, (256, 128, 64, 32, 16, 8))
        cos, sin = _rope_tables(pos0 + np.arange(t_len))
        nt = t_len // tr
        tab_spec = pl.BlockSpec((tr, LANES), lambda i: (i % nt, 0))
    col = lambda c: pl.BlockSpec((tr, d), lambda i: (i, c))
    out = pl.BlockSpec((tr, d), lambda i: (i, 0))
    return pl.pallas_call(
        functools.partial(_rope_body, d=d, q_scale=DA_QK ** -0.5 * math.log2(math.e)),
        grid=(m // tr,),
        in_specs=[col(0), col(1), col(2), tab_spec, tab_spec],
        out_specs=[out] * 5,
        out_shape=[
            jax.ShapeDtypeStruct((m, d), BF16),
            jax.ShapeDtypeStruct((m, d), F32),
            jax.ShapeDtypeStruct((m, d), BF16),
            jax.ShapeDtypeStruct((m, d), F32),
            jax.ShapeDtypeStruct((m, d), BF16),
        ],
        compiler_params=pltpu.CompilerParams(dimension_semantics=("parallel",)),
        name="rope",
    )(pa, pa, pa, cos, sin)


def _lam_value(lam_ref, lam_init):
    lv = lam_ref[...]
    s1 = jnp.sum(lv[0:1] * lv[1:2], axis=-1, keepdims=True)
    s2 = jnp.sum(lv[2:3] * lv[3:4], axis=-1, keepdims=True)
    return jnp.exp(s1) - jnp.exp(s2) + lam_init


def _subnorm(o, g, lam_init):
    o = o * lax.rsqrt(jnp.mean(o * o, axis=-1, keepdims=True) + SUBLN_EPS)
    return o * g * (1.0 - lam_init)


def _attn_body(lam_ref, q_ref, k_ref, v_ref, g_ref, o_ref, *, tq, nq, lam_init):
    lam = _lam_value(lam_ref, lam_init)
    g = g_ref[...]
    lane = lax.broadcasted_iota(jnp.int32, (tq, LANES), 1)
    row = lax.broadcasted_iota(jnp.int32, (2 * tq, tq), 0) % tq
    col = lax.broadcasted_iota(jnp.int32, (2 * tq, tq), 1)
    causal = col <= row
    contract_lanes = (((1,), (1,)), ((), ()))
    for i in range(nq):
        q = q_ref[i * tq:(i + 1) * tq, :]
        zero = jnp.zeros_like(q)
        q2 = jnp.concatenate([jnp.where(lane < DA_QK, q, zero), jnp.where(lane >= DA_QK, q, zero)], axis=0)
        lo = i * tq
        sd = lax.dot_general(q2, k_ref[lo:lo + tq, :], contract_lanes, preferred_element_type=F32)
        sd = jnp.where(causal, sd, -jnp.inf)
        m = jnp.max(sd, axis=-1, keepdims=True)
        if i > 0:
            sm = lax.dot_general(q2, k_ref[0:lo, :], contract_lanes, preferred_element_type=F32)
            m = jnp.maximum(m, jnp.max(sm, axis=-1, keepdims=True))
        pd = jnp.exp2(sd - m)
        l = jnp.sum(pd, axis=-1, keepdims=True)
        acc = jnp.dot(pd.astype(BF16), v_ref[lo:lo + tq, :], preferred_element_type=F32)
        if i > 0:
            pm = jnp.exp2(sm - m)
            l = l + jnp.sum(pm, axis=-1, keepdims=True)
            acc = acc + jnp.dot(pm.astype(BF16), v_ref[0:lo, :], preferred_element_type=F32)
        o = acc / l
        o = o[:tq] - lam * o[tq:]
        o_ref[lo:lo + tq, :] = _subnorm(o, g, lam_init)


def _attn_prompt(lamv, q_bf, k_bf, v_bf, g, b, t_len, heads, lam_init):
    m, d = q_bf.shape
    tq = _tile(t_len, (256, 128))
    seq = lambda: pl.BlockSpec((t_len, LANES), lambda bi, h: (bi, h))
    return pl.pallas_call(
        functools.partial(_attn_body, tq=tq, nq=t_len // tq, lam_init=lam_init),
        grid=(b, heads),
        in_specs=[pl.BlockSpec((8, LANES), lambda bi, h: (0, 0)), seq(), seq(), seq(),
                  pl.BlockSpec((1, LANES), lambda bi, h: (0, h))],
        out_specs=seq(),
        out_shape=jax.ShapeDtypeStruct((m, d), F32),
        compiler_params=pltpu.CompilerParams(dimension_semantics=("parallel", "parallel")),
        name="attn_prompt",
    )(lamv, q_bf, k_bf, v_bf, g)


def _dec_body(pt_ref, lam_ref, q_ref, kn_ref, vn_ref, g_ref, *refs, pp, heads, lam_init):
    k_refs = refs[:pp]
    v_refs = refs[pp:2 * pp]
    o_ref = refs[2 * pp]
    m_sc, l_sc, acc_sc = refs[2 * pp + 1:]
    j = pl.program_id(1)
    rows = 2 * heads
    q = q_ref[0]

    @pl.when(j == 0)
    def _():
        s = jnp.sum(q * kn_ref[0], axis=-1, keepdims=True)
        m_sc[...] = jnp.broadcast_to(s, (rows, LANES))
        l_sc[...] = jnp.ones((rows, LANES), F32)
        acc_sc[...] = vn_ref[0]

    kb = jnp.concatenate([r[0, 0].astype(BF16) for r in k_refs], axis=0)
    vb = jnp.concatenate([r[0, 0].astype(BF16) for r in v_refs], axis=0)
    n = kb.shape[0]
    s = lax.dot_general(q.astype(BF16), kb, (((1,), (1,)), ((), ())), preferred_element_type=F32)
    same_head = (lax.broadcasted_iota(jnp.int32, (rows, n), 1) % heads
                 == lax.broadcasted_iota(jnp.int32, (rows, n), 0) % heads)
    s = jnp.where(same_head, s, -jnp.inf)
    m_old = m_sc[...]
    m_new = jnp.maximum(m_old, jnp.max(s, axis=-1, keepdims=True))
    corr = jnp.exp2(m_old - m_new)
    p = jnp.exp2(s - m_new[:, 0:1])
    l_sc[...] = l_sc[...] * corr + jnp.sum(p, axis=-1, keepdims=True)
    acc_sc[...] = acc_sc[...] * corr + jnp.dot(p.astype(BF16), vb, preferred_element_type=F32)
    m_sc[...] = m_new

    @pl.when(j == pl.num_programs(1) - 1)
    def _():
        o = acc_sc[...] / l_sc[...]
        o = o[:heads] - _lam_value(lam_ref, lam_init) * o[heads:]
        o_ref[0] = _subnorm(o, g_ref[...], lam_init)


def _attn_decode(layer, lamv, q16, k16, v16, g, cache_k, cache_v, page_table, lam_init):
    bd, rows, _ = q16.shape
    heads = rows // 2
    n_pages = page_table.shape[1]
    page_rows = cache_k.shape[2]
    pp = _tile(n_pages, (8, 4, 2, 1))

    def cache_spec(pi):
        return pl.BlockSpec((1, 1, page_rows, LANES), lambda b, j, pt: (layer, pt[b, j * pp + pi], 0, 0))

    per_b = pl.BlockSpec((1, rows, LANES), lambda b, j, pt: (b, 0, 0))
    grid_spec = pltpu.PrefetchScalarGridSpec(
        num_scalar_prefetch=1,
        grid=(bd, n_pages // pp),
        in_specs=[pl.BlockSpec((8, LANES), lambda b, j, pt: (0, 0)), per_b, per_b, per_b,
                  pl.BlockSpec((heads, LANES), lambda b, j, pt: (0, 0))]
        + [cache_spec(pi) for pi in range(pp)] + [cache_spec(pi) for pi in range(pp)],
        out_specs=pl.BlockSpec((1, heads, LANES), lambda b, j, pt: (b, 0, 0)),
        scratch_shapes=[pltpu.VMEM((rows, LANES), F32)] * 3,
    )
    return pl.pallas_call(
        functools.partial(_dec_body, pp=pp, heads=heads, lam_init=lam_init),
        grid_spec=grid_spec,
        out_shape=jax.ShapeDtypeStruct((bd, heads, LANES), F32),
        compiler_params=pltpu.CompilerParams(dimension_semantics=("parallel", "arbitrary")),
        name="attn_decode",
    )(page_table, lamv, q16, k16, v16, g, *([cache_k] * pp), *([cache_v] * pp))


def _prep_body(p_ref, prev_ref, mu_ref, w0_ref, w2_ref, a0_ref, a2_ref, g2_ref, kk_ref, ka_ref,
               r_o, w_o, k_o, kk_o, a_o, v_o, g_o, *, d, tr, blocks_per_seq, seq):
    p = p_ref[...]
    if seq:
        i = pl.program_id(0)
        rolled = pltpu.roll(p, 1, 0)
        prev_row = prev_ref[7:8, :]
        prev_row = jnp.where(i % blocks_per_seq == 0, jnp.zeros_like(prev_row), prev_row)
        row = lax.broadcasted_iota(jnp.int32, (tr, 1), 0)
        prev = jnp.where(row == 0, prev_row, rolled)
    else:
        prev = prev_ref[...]
    xm = p + (prev - p) * mu_ref[...]
    r = xm[:, 0:d]
    k = xm[:, d:2 * d]
    v = xm[:, 2 * d:3 * d]
    o = 3 * d
    wd = xm[:, o:o + LORA_PAD[0]]
    ad = xm[:, o + LORA_PAD[0]:o + LORA_PAD[0] + LORA_PAD[1]]
    gd = xm[:, o + LORA_PAD[0] + LORA_PAD[1]:]
    u = w0_ref[...] + jnp.dot(jnp.tanh(wd).astype(BF16), w2_ref[...], preferred_element_type=F32)
    w_o[...] = jnp.exp(-math.exp(-0.5) * _sigmoid(u))
    a = _sigmoid(a0_ref[...] + jnp.dot(ad.astype(BF16), a2_ref[...], preferred_element_type=F32))
    g_o[...] = jnp.dot(_sigmoid(gd).astype(BF16), g2_ref[...], preferred_element_type=F32)
    r_o[...] = r
    v_o[...] = v
    a_o[...] = a
    kk_o[...] = k * kk_ref[...]
    k_o[...] = k * (1.0 + (a - 1.0) * ka_ref[...])


def _rwkv_prep(p_rw, prev, d, t_len, mu, w0, w2, a0, a2, g2, k_k, k_a):
    m, wpad = p_rw.shape
    seq = t_len > 1
    tr = _tile(t_len if seq else m, (256, 128, 64, 32, 16, 8))
    if seq:
        prev_arr = p_rw
        prev_spec = pl.BlockSpec((8, wpad), lambda i: (jnp.maximum(i * (tr // 8) - 1, 0), 0))
    else:
        prev_arr = prev
        prev_spec = pl.BlockSpec((tr, wpad), lambda i: (i, 0))
    full = lambda a: pl.BlockSpec(a.shape, lambda i: (0, 0))
    out = pl.BlockSpec((tr, d), lambda i: (i, 0))
    return pl.pallas_call(
        functools.partial(_prep_body, d=d, tr=tr, blocks_per_seq=max(t_len // tr, 1), seq=seq),
        grid=(m // tr,),
        in_specs=[pl.BlockSpec((tr, wpad), lambda i: (i, 0)), prev_spec, full(mu), full(w0), full(w2), full(a0),
                  full(a2), full(g2), full(k_k), full(k_a)],
        out_specs=[out] * 7,
        out_shape=[jax.ShapeDtypeStruct((m, d), F32)] * 7,
        compiler_params=pltpu.CompilerParams(dimension_semantics=("arbitrary",)),
        name="rwkv_prep",
    )(p_rw, prev_arr, mu, w0, w2, a0, a2, g2, k_k, k_a)


def _wkv_body(r_ref, w_ref, k_ref, kk_ref, a_ref, v_ref, s0_ref, rk_ref, lng_ref, lnb_ref,
              y_ref, st_ref, s_sc, y_sc, *, tb, n):
    @pl.when(pl.program_id(1) == 0)
    def _():
        s_sc[...] = s0_ref[...]

    def step(t, carry):
        r = r_ref[t]
        w = w_ref[t]
        k2 = k_ref[t]
        kkr = kk_ref[t]
        v = v_ref[t]
        norm = jnp.sqrt(jnp.sum(kkr * kkr, axis=0, keepdims=True))
        kk = kkr / jnp.maximum(norm, 1e-12)
        ka = kk * a_ref[t]
        nkk = -kk
        for vi in range(n):
            sv = s_sc[vi]
            sa = jnp.sum(sv * nkk, axis=0, keepdims=True)
            sv = sv * w + sa * ka + v[vi:vi + 1, :] * k2
            s_sc[vi] = sv
            y_sc[vi:vi + 1, :] = jnp.sum(sv * r, axis=0, keepdims=True)
        y = y_sc[...]
        mu = jnp.mean(y, axis=0, keepdims=True)
        yc = y - mu
        var = jnp.mean(yc * yc, axis=0, keepdims=True)
        bonus = jnp.sum(r * k2 * rk_ref[...], axis=0, keepdims=True) * v
        y_ref[t] = yc * lax.rsqrt(var + GN_EPS) * lng_ref[...] + lnb_ref[...] + bonus
        return carry

    lax.fori_loop(0, tb, step, 0)

    @pl.when(pl.program_id(1) == pl.num_programs(1) - 1)
    def _():
        st_ref[...] = s_sc[...]


def _wkv(r, w, k2, kk, a, v, s0, rk, lng, lnb):
    t_len, n, c = r.shape
    tb = _tile(t_len, (16, 8, 4, 2, 1))
    seqs = pl.BlockSpec((tb, n, LANES), lambda ci, ti: (ti, 0, ci))
    state = pl.BlockSpec((n, n, LANES), lambda ci, ti: (0, 0, ci))
    par = pl.BlockSpec((n, LANES), lambda ci, ti: (0, 0))
    return pl.pallas_call(
        functools.partial(_wkv_body, tb=tb, n=n),
        grid=(c // LANES, t_len // tb),
        in_specs=[seqs] * 6 + [state, par, par, par],
        out_specs=[seqs, state],
        out_shape=[jax.ShapeDtypeStruct((t_len, n, c), F32), jax.ShapeDtypeStruct((n, n, c), F32)],
        scratch_shapes=[pltpu.VMEM((n, n, LANES), F32), pltpu.VMEM((n, LANES), F32)],
        compiler_params=pltpu.CompilerParams(dimension_semantics=("parallel", "arbitrary")),
        name="wkv",
    )(r, w, k2, kk, a, v, s0, rk, lng, lnb)


def _wout_body(ga_ref, gb_ref, oa_ref, yb_ref, g_ref, x_ref, w_ref, lg_ref, lb_ref, h_ref, *, alpha):
    ob = yb_ref[...] * g_ref[...]
    merged = _sigmoid(ga_ref[...]) * oa_ref[...] + _sigmoid(gb_ref[...]) * ob
    acc = jnp.dot(merged.astype(BF16), w_ref[...], preferred_element_type=F32)
    h_ref[...] = _layer_norm(alpha * x_ref[...] + acc, lg_ref[...], lb_ref[...])


def _merge_wout(pa, o_a, y_b, g, x, w_out, ln_g, ln_b, alpha):
    m, d = x.shape
    tm = _tile(m, (256, 128))
    row = pl.BlockSpec((tm, d), lambda i: (i, 0))
    vec = pl.BlockSpec((1, d), lambda i: (0, 0))
    return pl.pallas_call(
        functools.partial(_wout_body, alpha=alpha),
        grid=(m // tm,),
        in_specs=[pl.BlockSpec((tm, d), lambda i: (i, 3)), pl.BlockSpec((tm, d), lambda i: (i, 4)), row, row, row, row,
                  pl.BlockSpec((d, d), lambda i: (0, 0)), vec, vec],
        out_specs=row,
        out_shape=jax.ShapeDtypeStruct((m, d), F32),
        compiler_params=pltpu.CompilerParams(dimension_semantics=("parallel",)),
        name="merge_wout",
    )(pa, pa, o_a, y_b, g, x, w_out, ln_g, ln_b)


def _ffn_body(h_ref, wu_ref, wd_ref, lg_ref, lb_ref, y_ref, hb_sc, acc_sc, *, alpha):
    j = pl.program_id(1)

    @pl.when(j == 0)
    def _():
        hb_sc[...] = h_ref[...].astype(BF16)
        acc_sc[...] = jnp.zeros_like(acc_sc)

    u = jnp.dot(hb_sc[...], wu_ref[...], preferred_element_type=F32)
    u = jnp.square(jnp.maximum(u, 0.0)).astype(BF16)
    acc_sc[...] += jnp.dot(u, wd_ref[...], preferred_element_type=F32)

    @pl.when(j == pl.num_programs(1) - 1)
    def _():
        y_ref[...] = _layer_norm(alpha * h_ref[...] + acc_sc[...], lg_ref[...], lb_ref[...])


def _ffn(h, w_up, w_down, ln_g, ln_b, alpha, tf=512):
    m, d = h.shape
    ff = w_up.shape[1]
    tm = _tile(m, (1024, 512, 256, 128))
    vec = pl.BlockSpec((1, d), lambda i, j: (0, 0))
    return pl.pallas_call(
        functools.partial(_ffn_body, alpha=alpha),
        grid=(m // tm, ff // tf),
        in_specs=[pl.BlockSpec((tm, d), lambda i, j: (i, 0)), pl.BlockSpec((d, tf), lambda i, j: (0, j)),
                  pl.BlockSpec((tf, d), lambda i, j: (j, 0)), vec, vec],
        out_specs=pl.BlockSpec((tm, d), lambda i, j: (i, 0)),
        out_shape=jax.ShapeDtypeStruct((m, d), F32),
        scratch_shapes=[pltpu.VMEM((tm, d), BF16), pltpu.VMEM((tm, d), F32)],
        compiler_params=pltpu.CompilerParams(dimension_semantics=("parallel", "arbitrary")),
        name="ffn",
    )(h, w_up, w_down, ln_g, ln_b)


def _rw_pad(a, d):
    o = 3 * d
    parts = [a[..., :o]]
    for width, padded in zip((DECAY_LORA, AAA_LORA, GATE_LORA), LORA_PAD):
        seg = a[..., o:o + width]
        parts.append(jnp.pad(seg, [(0, 0)] * (a.ndim - 1) + [(0, padded - width)]))
        o += width
    return jnp.concatenate(parts, axis=-1)


def _rw_unpad(a, d):
    o = 3 * d
    parts = [a[..., :o]]
    for width, padded in zip((DECAY_LORA, AAA_LORA, GATE_LORA), LORA_PAD):
        parts.append(a[..., o:o + width])
        o += padded
    return jnp.concatenate(parts, axis=-1)


def _pad_rows(a, rows):
    return jnp.pad(a, ((0, rows - a.shape[0]), (0, 0)))


def _to_chain(a, b, t_len, heads):
    n = a.shape[1] // heads
    return a.reshape(b, t_len, heads, n).transpose(1, 3, 0, 2).reshape(t_len, n, b * heads)


def _from_chain(a, b, heads):
    t_len, n, _ = a.shape
    return a.reshape(t_len, n, b, heads).transpose(2, 0, 3, 1).reshape(b * t_len, heads * n)


def _layer(layer, x, b, t_len, pos0, attend, wkv0, shift0, lam_init, alpha, lp):
    m, d = x.shape
    rw_heads = d // RW_HEAD
    pa = _proj(x, lp['w_a'])
    p_rw = _proj(x, lp['w_b'])
    q_bf, k_rot, k_bf, v, v_bf = _rope(pa, d, t_len, pos0)
    o_a = attend(q_bf, k_rot, k_bf, v, v_bf)

    r, w, k2, kk, a, vr, g = _rwkv_prep(p_rw, shift0, d, t_len, lp['rw_mu'], lp['rw_w0'], lp['rw_w2'], lp['rw_a0'],
                                        lp['rw_a2'], lp['rw_g2'], lp['rw_k_k'], lp['rw_k_a'])
    chain = lambda z: _to_chain(z, b, t_len, rw_heads)
    s0 = wkv0.transpose(2, 3, 0, 1).reshape(RW_HEAD, RW_HEAD, b * rw_heads)
    y_c, s_t = _wkv(chain(r), chain(w), chain(k2), chain(kk), chain(a), chain(vr), s0,
                    lp['rk_c'], lp['lng_c'], lp['lnb_c'])
    y_b = _from_chain(y_c, b, rw_heads)
    wkv = s_t.reshape(RW_HEAD, RW_HEAD, b, rw_heads).transpose(2, 3, 0, 1)
    shift = _rw_unpad(p_rw.reshape(b, t_len, -1)[:, -1], d)

    h = _merge_wout(pa, o_a, y_b, g, x, lp['w_out'], lp['ln1_g'], lp['ln1_b'], alpha)
    y = _ffn(h, lp['w_up'], lp['w_down'], lp['ln2_g'], lp['ln2_b'], alpha)
    return y, k_rot, v, wkv, shift


def kernel(x_prompt, x_sample, cache_k, cache_v, state_wkv, state_shift, page_table, w_in, lam_q1, lam_k1, lam_q2, lam_k2, da_norm_g, rw_mu, rw_w0, rw_w2, rw_a0, rw_a2, rw_g2, rw_k_k, rw_k_a, rw_r_k, rw_ln_g, rw_ln_b, w_out, ln1_g, ln1_b, w_up, w_down, ln2_g, ln2_b):
    b, t_len, d = x_prompt.shape
    bd, td, _ = x_sample.shape
    assert td == 1, "decode path handles one new token per sample"
    depth = w_in.shape[0]
    page = cache_k.shape[2]
    heads = cache_k.shape[3]
    assert cache_k.shape[4] == LANES and cache_v.shape[4] == LANES and heads * LANES == d
    assert t_len % page == 0
    past_len = page_table.shape[1] * page
    rw_heads = d // RW_HEAD
    alpha = (2 * depth) ** 0.25
    o_rw = 5 * d

    yp = x_prompt.reshape(b * t_len, d)
    ys = x_sample.reshape(bd, d)
    outs = [[] for _ in range(8)]
    for l in range(depth):
        lam_init = 0.8 - 0.6 * math.exp(-0.3 * l)
        row = lambda a: a[l].reshape(1, -1)
        lamv = _pad_rows(jnp.pad(jnp.stack([lam_q1[l], lam_k1[l], lam_q2[l], lam_k2[l]]), ((0, 0), (0, LANES - DA_QK))), 8)
        lane_heads = lambda a: jnp.tile(a.reshape(rw_heads, RW_HEAD).T, (1, LANES // rw_heads))
        lp = {
            'w_a': w_in[l][:, :o_rw].astype(BF16),
            'w_b': _rw_pad(w_in[l][:, o_rw:], d).astype(BF16),
            'rw_mu': _rw_pad(row(rw_mu), d),
            'rw_w0': row(rw_w0), 'rw_a0': row(rw_a0), 'rw_k_k': row(rw_k_k), 'rw_k_a': row(rw_k_a),
            'rw_w2': _pad_rows(rw_w2[l], LORA_PAD[0]).astype(BF16),
            'rw_a2': _pad_rows(rw_a2[l], LORA_PAD[1]).astype(BF16),
            'rw_g2': _pad_rows(rw_g2[l], LORA_PAD[2]).astype(BF16),
            'rk_c': lane_heads(rw_r_k[l]), 'lng_c': lane_heads(rw_ln_g[l]), 'lnb_c': lane_heads(rw_ln_b[l]),
            'w_out': w_out[l].astype(BF16), 'ln1_g': row(ln1_g), 'ln1_b': row(ln1_b),
            'w_up': w_up[l].astype(BF16), 'w_down': w_down[l].astype(BF16), 'ln2_g': row(ln2_g), 'ln2_b': row(ln2_b),
        }
        g_da = row(da_norm_g)

        def attend_p(q_bf, k_rot, k_bf, v, v_bf):
            return _attn_prompt(lamv, q_bf, k_bf, v_bf, g_da, b, t_len, heads, lam_init)

        def attend_s(q_bf, k_rot, k_bf, v, v_bf):
            q = q_bf.astype(F32).reshape(bd, heads, LANES)
            lane = jnp.arange(LANES)
            q16 = jnp.concatenate([jnp.where(lane < DA_QK, q, 0.0), jnp.where(lane >= DA_QK, q, 0.0)], axis=1)
            twice = lambda z: jnp.tile(z.reshape(bd, heads, LANES), (1, 2, 1))
            pool = cache_k.shape[1]
            o = _attn_decode(l, lamv, q16, twice(k_rot), twice(v), g_da.reshape(heads, LANES),
                             cache_k.reshape(depth, pool, page * heads, LANES),
                             cache_v.reshape(depth, pool, page * heads, LANES), page_table, lam_init)
            return o.reshape(bd, d)

        wkv0_p = jnp.zeros((b, rw_heads, RW_HEAD, RW_HEAD), F32)
        yp, kp, vp, wp, hp = _layer(l, yp, b, t_len, 0, attend_p, wkv0_p, None, lam_init, alpha, lp)
        ys, ks, vs, ws, hs = _layer(l, ys, bd, 1, past_len, attend_s, state_wkv[l], _rw_pad(state_shift[l], d),
                                    lam_init, alpha, lp)
        outs[0].append(kp.reshape(b * t_len // page, page, heads, LANES))
        outs[1].append(vp.reshape(b * t_len // page, page, heads, LANES))
        outs[2].append(ks.reshape(bd, 1, heads, LANES))
        outs[3].append(vs.reshape(bd, 1, heads, LANES))
        outs[4].append(wp)
        outs[5].append(ws)
        outs[6].append(hp)
        outs[7].append(hs)
    return (yp.reshape(b, t_len, d), ys.reshape(bd, 1, d)) + tuple(jnp.stack(o) for o in outs)
```

```python
import functools
import math

import numpy as np
import jax
import jax.numpy as jnp
from jax import lax
from jax.experimental import pallas as pl
from jax.experimental.pallas import tpu as pltpu

F32 = jnp.float32
BF16 = jnp.bfloat16

DA_QK = 64
ROPE_THETA = 10000.0
RW_HEAD = 64
DECAY_LORA = 64
AAA_LORA = 64
GATE_LORA = 160
LN_EPS = 1e-5
SUBLN_EPS = 1e-5
GN_EPS = 64e-5

LANES = 128
LORA_PAD = (LANES, LANES, 2 * LANES)


def _tile(m, prefs):
    for t in prefs:
        if m % t == 0:
            return t
    raise ValueError(f"no tile for {m} in {prefs}")


def _layer_norm(z, g, b):
    mu = jnp.mean(z, axis=-1, keepdims=True)
    zc = z - mu
    var = jnp.mean(zc * zc, axis=-1, keepdims=True)
    return zc * lax.rsqrt(var + LN_EPS) * g + b


def _sigmoid(z):
    return 1.0 / (1.0 + jnp.exp(-z))


def _proj_body(x_ref, w_ref, o_ref):
    o_ref[...] = jnp.dot(x_ref[...].astype(BF16), w_ref[...], preferred_element_type=F32)


def _proj(x, w, tn=512):
    m, d = x.shape
    n = w.shape[1]
    tm = _tile(m, (1024, 512, 256, 128))
    return pl.pallas_call(
        _proj_body,
        grid=(m // tm, n // tn),
        in_specs=[pl.BlockSpec((tm, d), lambda i, j: (i, 0)), pl.BlockSpec((d, tn), lambda i, j: (0, j))],
        out_specs=pl.BlockSpec((tm, tn), lambda i, j: (i, j)),
        out_shape=jax.ShapeDtypeStruct((m, n), F32),
        compiler_params=pltpu.CompilerParams(dimension_semantics=("parallel", "arbitrary")),
        name="proj",
    )(x, w)


def _rope_body(q_ref, k_ref, v_ref, cos_ref, sin_ref, qo_ref, ko_ref, kbo_ref, vo_ref, vbo_ref, *, d, q_scale):
    c = cos_ref[...]
    s = sin_ref[...]
    lane = lax.broadcasted_iota(jnp.int32, (1, LANES), 1)
    first = (lane % DA_QK) < (DA_QK // 2)
    for ci in range(d // LANES):
        sl = slice(ci * LANES, (ci + 1) * LANES)
        for src, scale, dsts in ((q_ref, q_scale, (None, qo_ref)), (k_ref, 1.0, (ko_ref, kbo_ref))):
            xc = src[:, sl]
            partner = jnp.where(first, pltpu.roll(xc, LANES - DA_QK // 2, 1), pltpu.roll(xc, DA_QK // 2, 1))
            y = xc * c + partner * s
            if dsts[0] is not None:
                dsts[0][:, sl] = y
            dsts[1][:, sl] = (y * scale).astype(BF16)
    v = v_ref[...]
    vo_ref[...] = v
    vbo_ref[...] = v.astype(BF16)


def _rope_tables(pos):
    half = DA_QK // 2
    inv = ROPE_THETA ** (-np.arange(half, dtype=np.float64) / half)
    ang = np.asarray(pos, np.float64)[:, None] * inv[None, :]
    cos = np.cos(ang)
    sin = np.sin(ang)
    cos64 = np.concatenate([cos, cos], -1)
    sin64 = np.concatenate([-sin, sin], -1)
    reps = LANES // DA_QK
    return (jnp.asarray(np.tile(cos64, (1, reps)), F32), jnp.asarray(np.tile(sin64, (1, reps)), F32))


def _rope(pa, d, t_len, pos0):
    m = pa.shape[0]
    if t_len == 1:
        tr = _tile(m, (256, 128, 64, 32, 16, 8))
        cos, sin = _rope_tables([pos0])
        tab_spec = pl.BlockSpec((1, LANES), lambda i: (0, 0))
    else:
        tr = _tile(t_len, (256, 128, 64, 32, 16, 8))
        cos, sin = _rope_tables(pos0 + np.arange(t_len))
        nt = t_len // tr
        tab_spec = pl.BlockSpec((tr, LANES), lambda i: (i % nt, 0))
    col = lambda c: pl.BlockSpec((tr, d), lambda i: (i, c))
    out = pl.BlockSpec((tr, d), lambda i: (i, 0))
    return pl.pallas_call(
        functools.partial(_rope_body, d=d, q_scale=DA_QK ** -0.5 * math.log2(math.e)),
        grid=(m // tr,),
        in_specs=[col(0), col(1), col(2), tab_spec, tab_spec],
        out_specs=[out] * 5,
        out_shape=[
            jax.ShapeDtypeStruct((m, d), BF16),
            jax.ShapeDtypeStruct((m, d), F32),
            jax.ShapeDtypeStruct((m, d), BF16),
            jax.ShapeDtypeStruct((m, d), F32),
            jax.ShapeDtypeStruct((m, d), BF16),
        ],
        compiler_params=pltpu.CompilerParams(dimension_semantics=("parallel",)),
        name="rope",
    )(pa, pa, pa, cos, sin)


def _lam_value(lam_ref, lam_init):
    lv = lam_ref[...]
    s1 = jnp.sum(lv[0:1] * lv[1:2], axis=-1, keepdims=True)
    s2 = jnp.sum(lv[2:3] * lv[3:4], axis=-1, keepdims=True)
    return jnp.exp(s1) - jnp.exp(s2) + lam_init


def _subnorm(o, g, lam_init):
    o = o * lax.rsqrt(jnp.mean(o * o, axis=-1, keepdims=True) + SUBLN_EPS)
    return o * g * (1.0 - lam_init)


def _attn_body(lam_ref, q_ref, k_ref, v_ref, g_ref, o_ref, *, tq, nq, lam_init):
    lam = _lam_value(lam_ref, lam_init)
    g = g_ref[...]
    lane = lax.broadcasted_iota(jnp.int32, (tq, LANES), 1)
    row = lax.broadcasted_iota(jnp.int32, (2 * tq, tq), 0) % tq
    col = lax.broadcasted_iota(jnp.int32, (2 * tq, tq), 1)
    causal = col <= row
    contract_lanes = (((1,), (1,)), ((), ()))
    for i in range(nq):
        q = q_ref[i * tq:(i + 1) * tq, :]
        zero = jnp.zeros_like(q)
        q2 = jnp.concatenate([jnp.where(lane < DA_QK, q, zero), jnp.where(lane >= DA_QK, q, zero)], axis=0)
        lo = i * tq
        sd = lax.dot_general(q2, k_ref[lo:lo + tq, :], contract_lanes, preferred_element_type=F32)
        sd = jnp.where(causal, sd, -jnp.inf)
        m = jnp.max(sd, axis=-1, keepdims=True)
        if i > 0:
            sm = lax.dot_general(q2, k_ref[0:lo, :], contract_lanes, preferred_element_type=F32)
            m = jnp.maximum(m, jnp.max(sm, axis=-1, keepdims=True))
        pd = jnp.exp2(sd - m)
        l = jnp.sum(pd, axis=-1, keepdims=True)
        acc = jnp.dot(pd.astype(BF16), v_ref[lo:lo + tq, :], preferred_element_type=F32)
        if i > 0:
            pm = jnp.exp2(sm - m)
            l = l + jnp.sum(pm, axis=-1, keepdims=True)
            acc = acc + jnp.dot(pm.astype(BF16), v_ref[0:lo, :], preferred_element_type=F32)
        o = acc / l
        o = o[:tq] - lam * o[tq:]
        o_ref[lo:lo + tq, :] = _subnorm(o, g, lam_init)


def _attn_prompt(lamv, q_bf, k_bf, v_bf, g, b, t_len, heads, lam_init):
    m, d = q_bf.shape
    tq = _tile(t_len, (256, 128))
    seq = lambda: pl.BlockSpec((t_len, LANES), lambda bi, h: (bi, h))
    return pl.pallas_call(
        functools.partial(_attn_body, tq=tq, nq=t_len // tq, lam_init=lam_init),
        grid=(b, heads),
        in_specs=[pl.BlockSpec((8, LANES), lambda bi, h: (0, 0)), seq(), seq(), seq(),
                  pl.BlockSpec((1, LANES), lambda bi, h: (0, h))],
        out_specs=seq(),
        out_shape=jax.ShapeDtypeStruct((m, d), F32),
        compiler_params=pltpu.CompilerParams(dimension_semantics=("parallel", "parallel")),
        name="attn_prompt",
    )(lamv, q_bf, k_bf, v_bf, g)


def _dec_body(pt_ref, lam_ref, q_ref, kn_ref, vn_ref, g_ref, *refs, pp, heads, lam_init):
    k_refs = refs[:pp]
    v_refs = refs[pp:2 * pp]
    o_ref = refs[2 * pp]
    m_sc, l_sc, acc_sc = refs[2 * pp + 1:]
    j = pl.program_id(1)
    rows = 2 * heads
    q = q_ref[0]

    @pl.when(j == 0)
    def _():
        s = jnp.sum(q * kn_ref[0], axis=-1, keepdims=True)
        m_sc[...] = jnp.broadcast_to(s, (rows, LANES))
        l_sc[...] = jnp.ones((rows, LANES), F32)
        acc_sc[...] = vn_ref[0]

    kb = jnp.concatenate([r[0, 0].astype(BF16) for r in k_refs], axis=0)
    vb = jnp.concatenate([r[0, 0].astype(BF16) for r in v_refs], axis=0)
    n = kb.shape[0]
    s = lax.dot_general(q.astype(BF16), kb, (((1,), (1,)), ((), ())), preferred_element_type=F32)
    same_head = (lax.broadcasted_iota(jnp.int32, (rows, n), 1) % heads
                 == lax.broadcasted_iota(jnp.int32, (rows, n), 0) % heads)
    s = jnp.where(same_head, s, -jnp.inf)
    m_old = m_sc[...]
    m_new = jnp.maximum(m_old, jnp.max(s, axis=-1, keepdims=True))
    corr = jnp.exp2(m_old - m_new)
    p = jnp.exp2(s - m_new[:, 0:1])
    l_sc[...] = l_sc[...] * corr + jnp.sum(p, axis=-1, keepdims=True)
    acc_sc[...] = acc_sc[...] * corr + jnp.dot(p.astype(BF16), vb, preferred_element_type=F32)
    m_sc[...] = m_new

    @pl.when(j == pl.num_programs(1) - 1)
    def _():
        o = acc_sc[...] / l_sc[...]
        o = o[:heads] - _lam_value(lam_ref, lam_init) * o[heads:]
        o_ref[0] = _subnorm(o, g_ref[...], lam_init)


def _attn_decode(layer, lamv, q16, k16, v16, g, cache_k, cache_v, page_table, lam_init):
    bd, rows, _ = q16.shape
    heads = rows // 2
    n_pages = page_table.shape[1]
    page_rows = cache_k.shape[2]
    pp = _tile(n_pages, (8, 4, 2, 1))

    def cache_spec(pi):
        return pl.BlockSpec((1, 1, page_rows, LANES), lambda b, j, pt: (layer, pt[b, j * pp + pi], 0, 0))

    per_b = pl.BlockSpec((1, rows, LANES), lambda b, j, pt: (b, 0, 0))
    grid_spec = pltpu.PrefetchScalarGridSpec(
        num_scalar_prefetch=1,
        grid=(bd, n_pages // pp),
        in_specs=[pl.BlockSpec((8, LANES), lambda b, j, pt: (0, 0)), per_b, per_b, per_b,
                  pl.BlockSpec((heads, LANES), lambda b, j, pt: (0, 0))]
        + [cache_spec(pi) for pi in range(pp)] + [cache_spec(pi) for pi in range(pp)],
        out_specs=pl.BlockSpec((1, heads, LANES), lambda b, j, pt: (b, 0, 0)),
        scratch_shapes=[pltpu.VMEM((rows, LANES), F32)] * 3,
    )
    return pl.pallas_call(
        functools.partial(_dec_body, pp=pp, heads=heads, lam_init=lam_init),
        grid_spec=grid_spec,
        out_shape=jax.ShapeDtypeStruct((bd, heads, LANES), F32),
        compiler_params=pltpu.CompilerParams(dimension_semantics=("parallel", "arbitrary")),
        name="attn_decode",
    )(page_table, lamv, q16, k16, v16, g, *([cache_k] * pp), *([cache_v] * pp))


def _prep_body(p_ref, prev_ref, mu_ref, w0_ref, w2_ref, a0_ref, a2_ref, g2_ref, kk_ref, ka_ref,
               r_o, w_o, k_o, kk_o, a_o, v_o, g_o, *, d, tr, blocks_per_seq, seq):
    p = p_ref[...]
    if seq:
        i = pl.program_id(0)
        rolled = pltpu.roll(p, 1, 0)
        prev_row = prev_ref[7:8, :]
        prev_row = jnp.where(i % blocks_per_seq == 0, jnp.zeros_like(prev_row), prev_row)
        row = lax.broadcasted_iota(jnp.int32, (tr, 1), 0)
        prev = jnp.where(row == 0, prev_row, rolled)
    else:
        prev = prev_ref[...]
    xm = p + (prev - p) * mu_ref[...]
    r = xm[:, 0:d]
    k = xm[:, d:2 * d]
    v = xm[:, 2 * d:3 * d]
    o = 3 * d
    wd = xm[:, o:o + LORA_PAD[0]]
    ad = xm[:, o + LORA_PAD[0]:o + LORA_PAD[0] + LORA_PAD[1]]
    gd = xm[:, o + LORA_PAD[0] + LORA_PAD[1]:]
    u = w0_ref[...] + jnp.dot(jnp.tanh(wd).astype(BF16), w2_ref[...], preferred_element_type=F32)
    w_o[...] = jnp.exp(-math.exp(-0.5) * _sigmoid(u))
    a = _sigmoid(a0_ref[...] + jnp.dot(ad.astype(BF16), a2_ref[...], preferred_element_type=F32))
    g_o[...] = jnp.dot(_sigmoid(gd).astype(BF16), g2_ref[...], preferred_element_type=F32)
    r_o[...] = r
    v_o[...] = v
    a_o[...] = a
    kk_o[...] = k * kk_ref[...]
    k_o[...] = k * (1.0 + (a - 1.0) * ka_ref[...])


def _rwkv_prep(p_rw, prev, d, t_len, mu, w0, w2, a0, a2, g2, k_k, k_a):
    m, wpad = p_rw.shape
    seq = t_len > 1
    tr = _tile(t_len if seq else m, (256, 128, 64, 32, 16, 8))
    out = pl.BlockSpec((tr, d), lambda i: (i, 0))
    if seq:
        nt = t_len // tr
        prev_arr = p_rw
        prev_spec = pl.BlockSpec((8, wpad), lambda i: (jnp.maximum(i * (tr // 8) - 1, 0), 0))
        rec_out = pl.BlockSpec((tr, d), lambda i: (i % nt, i // nt))
        rec_shape = jax.ShapeDtypeStruct((t_len, (m // t_len) * d), F32)
    else:
        prev_arr = prev
        prev_spec = pl.BlockSpec((tr, wpad), lambda i: (i, 0))
        rec_out = out
        rec_shape = jax.ShapeDtypeStruct((m, d), F32)
    full = lambda a: pl.BlockSpec(a.shape, lambda i: (0, 0))
    return pl.pallas_call(
        functools.partial(_prep_body, d=d, tr=tr, blocks_per_seq=max(t_len // tr, 1), seq=seq),
        grid=(m // tr,),
        in_specs=[pl.BlockSpec((tr, wpad), lambda i: (i, 0)), prev_spec, full(mu), full(w0), full(w2), full(a0),
                  full(a2), full(g2), full(k_k), full(k_a)],
        out_specs=[rec_out] * 6 + [out],
        out_shape=[rec_shape] * 6 + [jax.ShapeDtypeStruct((m, d), F32)],
        compiler_params=pltpu.CompilerParams(dimension_semantics=("arbitrary",)),
        name="rwkv_prep",
    )(p_rw, prev_arr, mu, w0, w2, a0, a2, g2, k_k, k_a)


def _wkv_step(s_sc, y_sc, r, w, k2, kkr, a, v, rk, lng, lnb, n):
    norm = jnp.sqrt(jnp.sum(kkr * kkr, axis=0, keepdims=True))
    kk = kkr / jnp.maximum(norm, 1e-12)
    ka = kk * a
    nkk = -kk
    for vi in range(n):
        sv = s_sc[vi]
        sa = jnp.sum(sv * nkk, axis=0, keepdims=True)
        sv = sv * w + sa * ka + v[vi:vi + 1, :] * k2
        s_sc[vi] = sv
        y_sc[vi:vi + 1, :] = jnp.sum(sv * r, axis=0, keepdims=True)
    y = y_sc[...]
    mu = jnp.mean(y, axis=0, keepdims=True)
    yc = y - mu
    var = jnp.mean(yc * yc, axis=0, keepdims=True)
    bonus = jnp.sum(r * k2 * rk, axis=0, keepdims=True) * v
    return yc * lax.rsqrt(var + GN_EPS) * lng + lnb + bonus


def _wkv_body(r_ref, w_ref, k_ref, kk_ref, a_ref, v_ref, s0_ref, rk_ref, lng_ref, lnb_ref,
              y_ref, st_ref, s_sc, y_sc, *, tb, n):
    @pl.when(pl.program_id(1) == 0)
    def _():
        s_sc[...] = s0_ref[...]

    def step(t, carry):
        y_ref[t] = _wkv_step(s_sc, y_sc, r_ref[t], w_ref[t], k_ref[t], kk_ref[t], a_ref[t], v_ref[t],
                             rk_ref[...], lng_ref[...], lnb_ref[...], n)
        return carry

    lax.fori_loop(0, tb, step, 0)

    @pl.when(pl.program_id(1) == pl.num_programs(1) - 1)
    def _():
        st_ref[...] = s_sc[...]


def _wkv_tok_body(r_ref, w_ref, k_ref, kk_ref, a_ref, v_ref, rk_ref, lng_ref, lnb_ref,
                  y_ref, st_ref, s_sc, y_sc, c_sc, yc_sc, *, tb, n):
    @pl.when(pl.program_id(1) == 0)
    def _():
        s_sc[...] = jnp.zeros_like(s_sc)

    ins = (r_ref, w_ref, k_ref, kk_ref, a_ref, v_ref)

    def to_chain(t, carry):
        for ai, ref in enumerate(ins):
            c_sc[ai, t] = ref[t].T
        return carry

    lax.fori_loop(0, tb, to_chain, 0)

    for par in range(2):
        rows = slice(par * n, (par + 1) * n)

        def step(t, carry, par=par, rows=rows):
            ops = [c_sc[ai, t, rows, :] for ai in range(len(ins))]
            yc_sc[t, rows, :] = _wkv_step(s_sc.at[par], y_sc, *ops, rk_ref[par], lng_ref[par], lnb_ref[par], n)
            return carry

        lax.fori_loop(0, tb, step, 0)

    def from_chain(t, carry):
        y_ref[t] = yc_sc[t].T
        return carry

    lax.fori_loop(0, tb, from_chain, 0)

    @pl.when(pl.program_id(1) == pl.num_programs(1) - 1)
    def _():
        st_ref[0] = s_sc[...]


def _wkv_tok(r, w, k2, kk, a, v, rk, lng, lnb):
    t_len, rows, _ = r.shape
    n = RW_HEAD
    tb = _tile(t_len, (16, 8, 4, 2, 1))
    seqs = pl.BlockSpec((tb, LANES, LANES), lambda ci, ti: (ti, ci, 0))
    par = pl.BlockSpec((2, n, LANES), lambda ci, ti: (0, 0, 0))
    groups = rows // LANES
    return pl.pallas_call(
        functools.partial(_wkv_tok_body, tb=tb, n=n),
        grid=(groups, t_len // tb),
        in_specs=[seqs] * 6 + [par, par, par],
        out_specs=[seqs, pl.BlockSpec((1, 2, n, n, LANES), lambda ci, ti: (ci, 0, 0, 0, 0))],
        out_shape=[jax.ShapeDtypeStruct((t_len, rows, LANES), F32),
                   jax.ShapeDtypeStruct((groups, 2, n, n, LANES), F32)],
        scratch_shapes=[pltpu.VMEM((2, n, n, LANES), F32), pltpu.VMEM((n, LANES), F32),
                        pltpu.VMEM((6, tb, LANES, LANES), F32), pltpu.VMEM((tb, LANES, LANES), F32)],
        compiler_params=pltpu.CompilerParams(dimension_semantics=("parallel", "arbitrary")),
        name="wkv_tok",
    )(r, w, k2, kk, a, v, rk, lng, lnb)


def _wkv(r, w, k2, kk, a, v, s0, rk, lng, lnb):
    t_len, n, c = r.shape
    tb = _tile(t_len, (16, 8, 4, 2, 1))
    seqs = pl.BlockSpec((tb, n, LANES), lambda ci, ti: (ti, 0, ci))
    state = pl.BlockSpec((n, n, LANES), lambda ci, ti: (0, 0, ci))
    par = pl.BlockSpec((n, LANES), lambda ci, ti: (0, 0))
    return pl.pallas_call(
        functools.partial(_wkv_body, tb=tb, n=n),
        grid=(c // LANES, t_len // tb),
        in_specs=[seqs] * 6 + [state, par, par, par],
        out_specs=[seqs, state],
        out_shape=[jax.ShapeDtypeStruct((t_len, n, c), F32), jax.ShapeDtypeStruct((n, n, c), F32)],
        scratch_shapes=[pltpu.VMEM((n, n, LANES), F32), pltpu.VMEM((n, LANES), F32)],
        compiler_params=pltpu.CompilerParams(dimension_semantics=("parallel", "arbitrary")),
        name="wkv",
    )(r, w, k2, kk, a, v, s0, rk, lng, lnb)


def _wout_body(ga_ref, gb_ref, oa_ref, yb_ref, g_ref, x_ref, w_ref, lg_ref, lb_ref, h_ref, *, alpha):
    ob = yb_ref[...] * g_ref[...]
    merged = _sigmoid(ga_ref[...]) * oa_ref[...] + _sigmoid(gb_ref[...]) * ob
    acc = jnp.dot(merged.astype(BF16), w_ref[...], preferred_element_type=F32)
    h_ref[...] = _layer_norm(alpha * x_ref[...] + acc, lg_ref[...], lb_ref[...])


def _merge_wout(pa, o_a, y_b, g, x, w_out, ln_g, ln_b, alpha, t_len):
    m, d = x.shape
    time_major = y_b.shape[0] != m
    tm = _tile(t_len if time_major else m, (256, 128))
    row = pl.BlockSpec((tm, d), lambda i: (i, 0))
    vec = pl.BlockSpec((1, d), lambda i: (0, 0))
    nt = t_len // tm if time_major else 1
    y_spec = pl.BlockSpec((tm, d), lambda i: (i % nt, i // nt)) if time_major else row
    return pl.pallas_call(
        functools.partial(_wout_body, alpha=alpha),
        grid=(m // tm,),
        in_specs=[pl.BlockSpec((tm, d), lambda i: (i, 3)), pl.BlockSpec((tm, d), lambda i: (i, 4)), row, y_spec, row, row,
                  pl.BlockSpec((d, d), lambda i: (0, 0)), vec, vec],
        out_specs=row,
        out_shape=jax.ShapeDtypeStruct((m, d), F32),
        compiler_params=pltpu.CompilerParams(dimension_semantics=("parallel",)),
        name="merge_wout",
    )(pa, pa, o_a, y_b, g, x, w_out, ln_g, ln_b)


def _ffn_body(h_ref, wu_ref, wd_ref, lg_ref, lb_ref, y_ref, hb_sc, acc_sc, *, alpha):
    j = pl.program_id(1)

    @pl.when(j == 0)
    def _():
        hb_sc[...] = h_ref[...].astype(BF16)
        acc_sc[...] = jnp.zeros_like(acc_sc)

    u = jnp.dot(hb_sc[...], wu_ref[...], preferred_element_type=F32)
    u = jnp.square(jnp.maximum(u, 0.0)).astype(BF16)
    acc_sc[...] += jnp.dot(u, wd_ref[...], preferred_element_type=F32)

    @pl.when(j == pl.num_programs(1) - 1)
    def _():
        y_ref[...] = _layer_norm(alpha * h_ref[...] + acc_sc[...], lg_ref[...], lb_ref[...])


def _ffn(h, w_up, w_down, ln_g, ln_b, alpha, tf=512):
    m, d = h.shape
    ff = w_up.shape[1]
    tm = _tile(m, (1024, 512, 256, 128))
    vec = pl.BlockSpec((1, d), lambda i, j: (0, 0))
    return pl.pallas_call(
        functools.partial(_ffn_body, alpha=alpha),
        grid=(m // tm, ff // tf),
        in_specs=[pl.BlockSpec((tm, d), lambda i, j: (i, 0)), pl.BlockSpec((d, tf), lambda i, j: (0, j)),
                  pl.BlockSpec((tf, d), lambda i, j: (j, 0)), vec, vec],
        out_specs=pl.BlockSpec((tm, d), lambda i, j: (i, 0)),
        out_shape=jax.ShapeDtypeStruct((m, d), F32),
        scratch_shapes=[pltpu.VMEM((tm, d), BF16), pltpu.VMEM((tm, d), F32)],
        compiler_params=pltpu.CompilerParams(dimension_semantics=("parallel", "arbitrary")),
        name="ffn",
    )(h, w_up, w_down, ln_g, ln_b)


def _rw_pad(a, d):
    o = 3 * d
    parts = [a[..., :o]]
    for width, padded in zip((DECAY_LORA, AAA_LORA, GATE_LORA), LORA_PAD):
        seg = a[..., o:o + width]
        parts.append(jnp.pad(seg, [(0, 0)] * (a.ndim - 1) + [(0, padded - width)]))
        o += width
    return jnp.concatenate(parts, axis=-1)


def _rw_unpad(a, d):
    o = 3 * d
    parts = [a[..., :o]]
    for width, padded in zip((DECAY_LORA, AAA_LORA, GATE_LORA), LORA_PAD):
        parts.append(a[..., o:o + width])
        o += padded
    return jnp.concatenate(parts, axis=-1)


def _pad_rows(a, rows):
    return jnp.pad(a, ((0, rows - a.shape[0]), (0, 0)))


def _to_chain(a, b, t_len, heads):
    n = a.shape[1] // heads
    return a.reshape(b, t_len, heads, n).transpose(1, 3, 0, 2).reshape(t_len, n, b * heads)


def _from_chain(a, b, heads):
    t_len, n, _ = a.shape
    return a.reshape(t_len, n, b, heads).transpose(2, 0, 3, 1).reshape(b * t_len, heads * n)


def _layer(layer, x, b, t_len, pos0, attend, wkv0, shift0, lam_init, alpha, lp):
    m, d = x.shape
    rw_heads = d // RW_HEAD
    pa = _proj(x, lp['w_a'])
    p_rw = _proj(x, lp['w_b'])
    q_bf, k_rot, k_bf, v, v_bf = _rope(pa, d, t_len, pos0)
    o_a = attend(q_bf, k_rot, k_bf, v, v_bf)

    r, w, k2, kk, a, vr, g = _rwkv_prep(p_rw, shift0, d, t_len, lp['rw_mu'], lp['rw_w0'], lp['rw_w2'], lp['rw_a0'],
                                        lp['rw_a2'], lp['rw_g2'], lp['rw_k_k'], lp['rw_k_a'])
    if t_len > 1:
        rows = lambda z: z.reshape(t_len, b * d // LANES, LANES)
        y_r, s_t = _wkv_tok(rows(r), rows(w), rows(k2), rows(kk), rows(a), rows(vr),
                            lp['rk_t'], lp['lng_t'], lp['lnb_t'])
        y_b = y_r.reshape(t_len, b * d)
        pairs = rw_heads // 2
        wkv = (s_t.reshape(-1, 2, RW_HEAD, RW_HEAD, LANES // pairs, pairs)
               .transpose(0, 4, 5, 1, 2, 3).reshape(b, rw_heads, RW_HEAD, RW_HEAD))
    else:
        chain = lambda z: _to_chain(z, b, t_len, rw_heads)
        s0 = wkv0.transpose(2, 3, 0, 1).reshape(RW_HEAD, RW_HEAD, b * rw_heads)
        y_c, s_t = _wkv(chain(r), chain(w), chain(k2), chain(kk), chain(a), chain(vr), s0,
                        lp['rk_c'], lp['lng_c'], lp['lnb_c'])
        y_b = _from_chain(y_c, b, rw_heads)
        wkv = s_t.reshape(RW_HEAD, RW_HEAD, b, rw_heads).transpose(2, 3, 0, 1)
    shift = _rw_unpad(p_rw.reshape(b, t_len, -1)[:, -1], d)

    h = _merge_wout(pa, o_a, y_b, g, x, lp['w_out'], lp['ln1_g'], lp['ln1_b'], alpha, t_len)
    y = _ffn(h, lp['w_up'], lp['w_down'], lp['ln2_g'], lp['ln2_b'], alpha)
    return y, k_rot, v, wkv, shift


def kernel(x_prompt, x_sample, cache_k, cache_v, state_wkv, state_shift, page_table, w_in, lam_q1, lam_k1, lam_q2, lam_k2, da_norm_g, rw_mu, rw_w0, rw_w2, rw_a0, rw_a2, rw_g2, rw_k_k, rw_k_a, rw_r_k, rw_ln_g, rw_ln_b, w_out, ln1_g, ln1_b, w_up, w_down, ln2_g, ln2_b):
    b, t_len, d = x_prompt.shape
    bd, td, _ = x_sample.shape
    assert td == 1, "decode path handles one new token per sample"
    depth = w_in.shape[0]
    page = cache_k.shape[2]
    heads = cache_k.shape[3]
    assert cache_k.shape[4] == LANES and cache_v.shape[4] == LANES and heads * LANES == d
    assert t_len % page == 0
    past_len = page_table.shape[1] * page
    rw_heads = d // RW_HEAD
    alpha = (2 * depth) ** 0.25
    o_rw = 5 * d

    yp = x_prompt.reshape(b * t_len, d)
    ys = x_sample.reshape(bd, d)
    outs = [[] for _ in range(8)]
    for l in range(depth):
        lam_init = 0.8 - 0.6 * math.exp(-0.3 * l)
        row = lambda a: a[l].reshape(1, -1)
        lamv = _pad_rows(jnp.pad(jnp.stack([lam_q1[l], lam_k1[l], lam_q2[l], lam_k2[l]]), ((0, 0), (0, LANES - DA_QK))), 8)
        lane_heads = lambda a: jnp.tile(a.reshape(rw_heads, RW_HEAD).T, (1, LANES // rw_heads))
        tok_head = np.array([[2 * (ln % (rw_heads // 2)) + par for ln in range(LANES)] for par in range(2)])
        tok_lane_heads = lambda a: jnp.stack([a.reshape(rw_heads, RW_HEAD).T[:, tok_head[par]] for par in range(2)])
        lp = {
            'w_a': w_in[l][:, :o_rw].astype(BF16),
            'w_b': _rw_pad(w_in[l][:, o_rw:], d).astype(BF16),
            'rw_mu': _rw_pad(row(rw_mu), d),
            'rw_w0': row(rw_w0), 'rw_a0': row(rw_a0), 'rw_k_k': row(rw_k_k), 'rw_k_a': row(rw_k_a),
            'rw_w2': _pad_rows(rw_w2[l], LORA_PAD[0]).astype(BF16),
            'rw_a2': _pad_rows(rw_a2[l], LORA_PAD[1]).astype(BF16),
            'rw_g2': _pad_rows(rw_g2[l], LORA_PAD[2]).astype(BF16),
            'rk_c': lane_heads(rw_r_k[l]), 'lng_c': lane_heads(rw_ln_g[l]), 'lnb_c': lane_heads(rw_ln_b[l]),
            'rk_t': tok_lane_heads(rw_r_k[l]), 'lng_t': tok_lane_heads(rw_ln_g[l]), 'lnb_t': tok_lane_heads(rw_ln_b[l]),
            'w_out': w_out[l].astype(BF16), 'ln1_g': row(ln1_g), 'ln1_b': row(ln1_b),
            'w_up': w_up[l].astype(BF16), 'w_down': w_down[l].astype(BF16), 'ln2_g': row(ln2_g), 'ln2_b': row(ln2_b),
        }
        g_da = row(da_norm_g)

        def attend_p(q_bf, k_rot, k_bf, v, v_bf):
            return _attn_prompt(lamv, q_bf, k_bf, v_bf, g_da, b, t_len, heads, lam_init)

        def attend_s(q_bf, k_rot, k_bf, v, v_bf):
            q = q_bf.astype(F32).reshape(bd, heads, LANES)
            lane = jnp.arange(LANES)
            q16 = jnp.concatenate([jnp.where(lane < DA_QK, q, 0.0), jnp.where(lane >= DA_QK, q, 0.0)], axis=1)
            twice = lambda z: jnp.tile(z.reshape(bd, heads, LANES), (1, 2, 1))
            pool = cache_k.shape[1]
            o = _attn_decode(l, lamv, q16, twice(k_rot), twice(v), g_da.reshape(heads, LANES),
                             cache_k.reshape(depth, pool, page * heads, LANES),
                             cache_v.reshape(depth, pool, page * heads, LANES), page_table, lam_init)
            return o.reshape(bd, d)

        wkv0_p = jnp.zeros((b, rw_heads, RW_HEAD, RW_HEAD), F32)
        yp, kp, vp, wp, hp = _layer(l, yp, b, t_len, 0, attend_p, wkv0_p, None, lam_init, alpha, lp)
        ys, ks, vs, ws, hs = _layer(l, ys, bd, 1, past_len, attend_s, state_wkv[l], _rw_pad(state_shift[l], d),
                                    lam_init, alpha, lp)
        outs[0].append(kp.reshape(b * t_len // page, page, heads, LANES))
        outs[1].append(vp.reshape(b * t_len // page, page, heads, LANES))
        outs[2].append(ks.reshape(bd, 1, heads, LANES))
        outs[3].append(vs.reshape(bd, 1, heads, LANES))
        outs[4].append(wp)
        outs[5].append(ws)
        outs[6].append(hp)
        outs[7].append(hs)
    return (yp.reshape(b, t_len, d), ys.reshape(bd, 1, d)) + tuple(jnp.stack(o) for o in outs)
```

```python
import functools
import math

import numpy as np
import jax
import jax.numpy as jnp
from jax import lax
from jax.experimental import pallas as pl
from jax.experimental.pallas import tpu as pltpu

F32 = jnp.float32
BF16 = jnp.bfloat16

DA_QK = 64
ROPE_THETA = 10000.0
RW_HEAD = 64
DECAY_LORA = 64
AAA_LORA = 64
GATE_LORA = 160
LN_EPS = 1e-5
SUBLN_EPS = 1e-5
GN_EPS = 64e-5

LANES = 128
LORA_PAD = (LANES, LANES, 2 * LANES)


def _tile(m, prefs):
    for t in prefs:
        if m % t == 0:
            return t
    raise ValueError(f"no tile for {m} in {prefs}")


def _layer_norm(z, g, b):
    mu = jnp.mean(z, axis=-1, keepdims=True)
    zc = z - mu
    var = jnp.mean(zc * zc, axis=-1, keepdims=True)
    return zc * lax.rsqrt(var + LN_EPS) * g + b


def _sigmoid(z):
    return 1.0 / (1.0 + jnp.exp(-z))


def _proj_body(x_ref, w_ref, o_ref):
    o_ref[...] = jnp.dot(x_ref[...].astype(BF16), w_ref[...], preferred_element_type=F32)


def _proj(x, w):
    m, d = x.shape
    n = w.shape[1]
    tm = _tile(m, (256, 128))
    return pl.pallas_call(
        _proj_body,
        grid=(m // tm,),
        in_specs=[pl.BlockSpec((tm, d), lambda i: (i, 0)), pl.BlockSpec((d, n), lambda i: (0, 0))],
        out_specs=pl.BlockSpec((tm, n), lambda i: (i, 0)),
        out_shape=jax.ShapeDtypeStruct((m, n), F32),
        compiler_params=pltpu.CompilerParams(dimension_semantics=("parallel",)),
        name="proj",
    )(x, w)


def _rope_body(q_ref, k_ref, v_ref, cos_ref, sin_ref, qo_ref, ko_ref, kbo_ref, vo_ref, vbo_ref, *, d, q_scale):
    c = cos_ref[...]
    s = sin_ref[...]
    lane = lax.broadcasted_iota(jnp.int32, (1, LANES), 1)
    first = (lane % DA_QK) < (DA_QK // 2)
    for ci in range(d // LANES):
        sl = slice(ci * LANES, (ci + 1) * LANES)
        for src, scale, dsts in ((q_ref, q_scale, (None, qo_ref)), (k_ref, 1.0, (ko_ref, kbo_ref))):
            xc = src[:, sl]
            partner = jnp.where(first, pltpu.roll(xc, LANES - DA_QK // 2, 1), pltpu.roll(xc, DA_QK // 2, 1))
            y = xc * c + partner * s
            if dsts[0] is not None:
                dsts[0][:, sl] = y
            dsts[1][:, sl] = (y * scale).astype(BF16)
    v = v_ref[...]
    vo_ref[...] = v
    vbo_ref[...] = v.astype(BF16)


def _rope_tables(pos):
    half = DA_QK // 2
    inv = ROPE_THETA ** (-np.arange(half, dtype=np.float64) / half)
    ang = np.asarray(pos, np.float64)[:, None] * inv[None, :]
    cos = np.cos(ang)
    sin = np.sin(ang)
    cos64 = np.concatenate([cos, cos], -1)
    sin64 = np.concatenate([-sin, sin], -1)
    reps = LANES // DA_QK
    return (jnp.asarray(np.tile(cos64, (1, reps)), F32), jnp.asarray(np.tile(sin64, (1, reps)), F32))


def _rope(pa, d, t_len, pos0):
    m = pa.shape[0]
    if t_len == 1:
        tr = _tile(m, (256, 128, 64, 32, 16, 8))
        cos, sin = _rope_tables([pos0])
        tab_spec = pl.BlockSpec((1, LANES), lambda i: (0, 0))
    else:
        tr = _tile(t_len, (256, 128, 64, 32, 16, 8))
        cos, sin = _rope_tables(pos0 + np.arange(t_len))
        nt = t_len // tr
        tab_spec = pl.BlockSpec((tr, LANES), lambda i: (i % nt, 0))
    col = lambda c: pl.BlockSpec((tr, d), lambda i: (i, c))
    out = pl.BlockSpec((tr, d), lambda i: (i, 0))
    return pl.pallas_call(
        functools.partial(_rope_body, d=d, q_scale=DA_QK ** -0.5 * math.log2(math.e)),
        grid=(m // tr,),
        in_specs=[col(0), col(1), col(2), tab_spec, tab_spec],
        out_specs=[out] * 5,
        out_shape=[
            jax.ShapeDtypeStruct((m, d), BF16),
            jax.ShapeDtypeStruct((m, d), F32),
            jax.ShapeDtypeStruct((m, d), BF16),
            jax.ShapeDtypeStruct((m, d), F32),
            jax.ShapeDtypeStruct((m, d), BF16),
        ],
        compiler_params=pltpu.CompilerParams(dimension_semantics=("parallel",)),
        name="rope",
    )(pa, pa, pa, cos, sin)


def _lam_value(lam_ref, lam_init):
    lv = lam_ref[...]
    s1 = jnp.sum(lv[0:1] * lv[1:2], axis=-1, keepdims=True)
    s2 = jnp.sum(lv[2:3] * lv[3:4], axis=-1, keepdims=True)
    return jnp.exp(s1) - jnp.exp(s2) + lam_init


def _subnorm(o, g, lam_init):
    o = o * lax.rsqrt(jnp.mean(o * o, axis=-1, keepdims=True) + SUBLN_EPS)
    return o * g * (1.0 - lam_init)


def _attn_body(lam_ref, q_ref, k_ref, v_ref, g_ref, o_ref, *, tq, nq, lam_init):
    lam = _lam_value(lam_ref, lam_init)
    g = g_ref[...]
    lane = lax.broadcasted_iota(jnp.int32, (tq, LANES), 1)
    row = lax.broadcasted_iota(jnp.int32, (2 * tq, tq), 0) % tq
    col = lax.broadcasted_iota(jnp.int32, (2 * tq, tq), 1)
    causal = col <= row
    contract_lanes = (((1,), (1,)), ((), ()))
    for i in range(nq):
        q = q_ref[i * tq:(i + 1) * tq, :]
        zero = jnp.zeros_like(q)
        q2 = jnp.concatenate([jnp.where(lane < DA_QK, q, zero), jnp.where(lane >= DA_QK, q, zero)], axis=0)
        lo = i * tq
        sd = lax.dot_general(q2, k_ref[lo:lo + tq, :], contract_lanes, preferred_element_type=F32)
        sd = jnp.where(causal, sd, -jnp.inf)
        m = jnp.max(sd, axis=-1, keepdims=True)
        if i > 0:
            sm = lax.dot_general(q2, k_ref[0:lo, :], contract_lanes, preferred_element_type=F32)
            m = jnp.maximum(m, jnp.max(sm, axis=-1, keepdims=True))
        pd = jnp.exp2(sd - m)
        l = jnp.sum(pd, axis=-1, keepdims=True)
        acc = jnp.dot(pd.astype(BF16), v_ref[lo:lo + tq, :], preferred_element_type=F32)
        if i > 0:
            pm = jnp.exp2(sm - m)
            l = l + jnp.sum(pm, axis=-1, keepdims=True)
            acc = acc + jnp.dot(pm.astype(BF16), v_ref[0:lo, :], preferred_element_type=F32)
        o = acc / l
        o = o[:tq] - lam * o[tq:]
        o_ref[lo:lo + tq, :] = _subnorm(o, g, lam_init)


def _attn_prompt(lamv, q_bf, k_bf, v_bf, g, b, t_len, heads, lam_init):
    m, d = q_bf.shape
    tq = _tile(t_len, (256, 128))
    seq = lambda: pl.BlockSpec((t_len, LANES), lambda bi, h: (bi, h))
    return pl.pallas_call(
        functools.partial(_attn_body, tq=tq, nq=t_len // tq, lam_init=lam_init),
        grid=(b, heads),
        in_specs=[pl.BlockSpec((8, LANES), lambda bi, h: (0, 0)), seq(), seq(), seq(),
                  pl.BlockSpec((1, LANES), lambda bi, h: (0, h))],
        out_specs=seq(),
        out_shape=jax.ShapeDtypeStruct((m, d), F32),
        compiler_params=pltpu.CompilerParams(dimension_semantics=("parallel", "parallel")),
        name="attn_prompt",
    )(lamv, q_bf, k_bf, v_bf, g)


def _dec_body(pt_ref, lam_ref, q_ref, kn_ref, vn_ref, g_ref, *refs, pp, heads, lam_init):
    k_refs = refs[:pp]
    v_refs = refs[pp:2 * pp]
    o_ref = refs[2 * pp]
    m_sc, l_sc, acc_sc = refs[2 * pp + 1:]
    j = pl.program_id(1)
    rows = 2 * heads
    q = q_ref[0]

    @pl.when(j == 0)
    def _():
        s = jnp.sum(q * kn_ref[0], axis=-1, keepdims=True)
        m_sc[...] = jnp.broadcast_to(s, (rows, LANES))
        l_sc[...] = jnp.ones((rows, LANES), F32)
        acc_sc[...] = vn_ref[0]

    kb = jnp.concatenate([r[0, 0].astype(BF16) for r in k_refs], axis=0)
    vb = jnp.concatenate([r[0, 0].astype(BF16) for r in v_refs], axis=0)
    n = kb.shape[0]
    s = lax.dot_general(q.astype(BF16), kb, (((1,), (1,)), ((), ())), preferred_element_type=F32)
    same_head = (lax.broadcasted_iota(jnp.int32, (rows, n), 1) % heads
                 == lax.broadcasted_iota(jnp.int32, (rows, n), 0) % heads)
    s = jnp.where(same_head, s, -jnp.inf)
    m_old = m_sc[...]
    m_new = jnp.maximum(m_old, jnp.max(s, axis=-1, keepdims=True))
    corr = jnp.exp2(m_old - m_new)
    p = jnp.exp2(s - m_new[:, 0:1])
    l_sc[...] = l_sc[...] * corr + jnp.sum(p, axis=-1, keepdims=True)
    acc_sc[...] = acc_sc[...] * corr + jnp.dot(p.astype(BF16), vb, preferred_element_type=F32)
    m_sc[...] = m_new

    @pl.when(j == pl.num_programs(1) - 1)
    def _():
        o = acc_sc[...] / l_sc[...]
        o = o[:heads] - _lam_value(lam_ref, lam_init) * o[heads:]
        o_ref[0] = _subnorm(o, g_ref[...], lam_init)


def _attn_decode(layer, lamv, q16, k16, v16, g, cache_k, cache_v, page_table, lam_init):
    bd, rows, _ = q16.shape
    heads = rows // 2
    n_pages = page_table.shape[1]
    page_rows = cache_k.shape[2]
    pp = _tile(n_pages, (8, 4, 2, 1))

    def cache_spec(pi):
        return pl.BlockSpec((1, 1, page_rows, LANES), lambda b, j, pt: (layer, pt[b, j * pp + pi], 0, 0))

    per_b = pl.BlockSpec((1, rows, LANES), lambda b, j, pt: (b, 0, 0))
    grid_spec = pltpu.PrefetchScalarGridSpec(
        num_scalar_prefetch=1,
        grid=(bd, n_pages // pp),
        in_specs=[pl.BlockSpec((8, LANES), lambda b, j, pt: (0, 0)), per_b, per_b, per_b,
                  pl.BlockSpec((heads, LANES), lambda b, j, pt: (0, 0))]
        + [cache_spec(pi) for pi in range(pp)] + [cache_spec(pi) for pi in range(pp)],
        out_specs=pl.BlockSpec((1, heads, LANES), lambda b, j, pt: (b, 0, 0)),
        scratch_shapes=[pltpu.VMEM((rows, LANES), F32)] * 3,
    )
    return pl.pallas_call(
        functools.partial(_dec_body, pp=pp, heads=heads, lam_init=lam_init),
        grid_spec=grid_spec,
        out_shape=jax.ShapeDtypeStruct((bd, heads, LANES), F32),
        compiler_params=pltpu.CompilerParams(dimension_semantics=("parallel", "arbitrary")),
        name="attn_decode",
    )(page_table, lamv, q16, k16, v16, g, *([cache_k] * pp), *([cache_v] * pp))


def _prep_body(p_ref, prev_ref, mu_ref, w0_ref, w2_ref, a0_ref, a2_ref, g2_ref, kk_ref, ka_ref,
               r_o, w_o, k_o, kk_o, a_o, v_o, g_o, *, d, tr, blocks_per_seq, seq):
    p = p_ref[...]
    if seq:
        i = pl.program_id(0)
        rolled = pltpu.roll(p, 1, 0)
        prev_row = prev_ref[7:8, :]
        prev_row = jnp.where(i % blocks_per_seq == 0, jnp.zeros_like(prev_row), prev_row)
        row = lax.broadcasted_iota(jnp.int32, (tr, 1), 0)
        prev = jnp.where(row == 0, prev_row, rolled)
    else:
        prev = prev_ref[...]
    xm = p + (prev - p) * mu_ref[...]
    r = xm[:, 0:d]
    k = xm[:, d:2 * d]
    v = xm[:, 2 * d:3 * d]
    o = 3 * d
    wd = xm[:, o:o + LORA_PAD[0]]
    ad = xm[:, o + LORA_PAD[0]:o + LORA_PAD[0] + LORA_PAD[1]]
    gd = xm[:, o + LORA_PAD[0] + LORA_PAD[1]:]
    u = w0_ref[...] + jnp.dot(jnp.tanh(wd).astype(BF16), w2_ref[...], preferred_element_type=F32)
    w_o[...] = jnp.exp(-math.exp(-0.5) * _sigmoid(u))
    a = _sigmoid(a0_ref[...] + jnp.dot(ad.astype(BF16), a2_ref[...], preferred_element_type=F32))
    g_o[...] = jnp.dot(_sigmoid(gd).astype(BF16), g2_ref[...], preferred_element_type=F32)
    r_o[...] = r
    v_o[...] = v
    a_o[...] = a
    kk_o[...] = k * kk_ref[...]
    k_o[...] = k * (1.0 + (a - 1.0) * ka_ref[...])


def _rwkv_prep(p_rw, prev, d, t_len, mu, w0, w2, a0, a2, g2, k_k, k_a):
    m, wpad = p_rw.shape
    seq = t_len > 1
    tr = _tile(t_len if seq else m, (256, 128, 64, 32, 16, 8))
    out = pl.BlockSpec((tr, d), lambda i: (i, 0))
    if seq:
        nt = t_len // tr
        prev_arr = p_rw
        prev_spec = pl.BlockSpec((8, wpad), lambda i: (jnp.maximum(i * (tr // 8) - 1, 0), 0))
        rec_out = pl.BlockSpec((tr, d), lambda i: (i % nt, i // nt))
        rec_shape = jax.ShapeDtypeStruct((t_len, (m // t_len) * d), F32)
    else:
        prev_arr = prev
        prev_spec = pl.BlockSpec((tr, wpad), lambda i: (i, 0))
        rec_out = out
        rec_shape = jax.ShapeDtypeStruct((m, d), F32)
    full = lambda a: pl.BlockSpec(a.shape, lambda i: (0, 0))
    return pl.pallas_call(
        functools.partial(_prep_body, d=d, tr=tr, blocks_per_seq=max(t_len // tr, 1), seq=seq),
        grid=(m // tr,),
        in_specs=[pl.BlockSpec((tr, wpad), lambda i: (i, 0)), prev_spec, full(mu), full(w0), full(w2), full(a0),
                  full(a2), full(g2), full(k_k), full(k_a)],
        out_specs=[rec_out] * 6 + [out],
        out_shape=[rec_shape] * 6 + [jax.ShapeDtypeStruct((m, d), F32)],
        compiler_params=pltpu.CompilerParams(dimension_semantics=("arbitrary",)),
        name="rwkv_prep",
    )(p_rw, prev_arr, mu, w0, w2, a0, a2, g2, k_k, k_a)


def _wkv_step(s_sc, y_sc, r, w, k2, kkr, a, v, rk, lng, lnb, n):
    norm = jnp.sqrt(jnp.sum(kkr * kkr, axis=0, keepdims=True))
    kk = kkr / jnp.maximum(norm, 1e-12)
    ka = kk * a
    nkk = -kk
    for vi in range(n):
        sv = s_sc[vi]
        sa = jnp.sum(sv * nkk, axis=0, keepdims=True)
        sv = sv * w + sa * ka + v[vi:vi + 1, :] * k2
        s_sc[vi] = sv
        y_sc[vi:vi + 1, :] = jnp.sum(sv * r, axis=0, keepdims=True)
    y = y_sc[...]
    mu = jnp.mean(y, axis=0, keepdims=True)
    yc = y - mu
    var = jnp.mean(yc * yc, axis=0, keepdims=True)
    bonus = jnp.sum(r * k2 * rk, axis=0, keepdims=True) * v
    return yc * lax.rsqrt(var + GN_EPS) * lng + lnb + bonus


def _wkv_body(r_ref, w_ref, k_ref, kk_ref, a_ref, v_ref, s0_ref, rk_ref, lng_ref, lnb_ref,
              y_ref, st_ref, s_sc, y_sc, *, tb, n):
    @pl.when(pl.program_id(1) == 0)
    def _():
        s_sc[...] = s0_ref[...]

    def step(t, carry):
        y_ref[t] = _wkv_step(s_sc, y_sc, r_ref[t], w_ref[t], k_ref[t], kk_ref[t], a_ref[t], v_ref[t],
                             rk_ref[...], lng_ref[...], lnb_ref[...], n)
        return carry

    lax.fori_loop(0, tb, step, 0)

    @pl.when(pl.program_id(1) == pl.num_programs(1) - 1)
    def _():
        st_ref[...] = s_sc[...]


def _wkv_step_kmajor(s_sc, tmp_sc, rows_of, r, k2, kkr, a, v, rk, lng, lnb, n):
    norm = jnp.sqrt(jnp.sum(kkr * kkr, axis=0, keepdims=True))
    kk = kkr / jnp.maximum(norm, 1e-12)
    tmp_sc[0] = -kk
    tmp_sc[1] = kk * a
    n_acc = 4
    parts = [None] * n_acc
    for k in range(n):
        t = s_sc[k] * tmp_sc[0, k:k + 1, :]
        parts[k % n_acc] = t if parts[k % n_acc] is None else parts[k % n_acc] + t
    sa = (parts[0] + parts[1]) + (parts[2] + parts[3])
    parts = [None] * n_acc
    for k in range(n):
        sk = s_sc[k] * rows_of('w', k) + sa * tmp_sc[1, k:k + 1, :] + v * rows_of('k', k)
        s_sc[k] = sk
        t = sk * rows_of('r', k)
        parts[k % n_acc] = t if parts[k % n_acc] is None else parts[k % n_acc] + t
    y = (parts[0] + parts[1]) + (parts[2] + parts[3])
    mu = jnp.mean(y, axis=0, keepdims=True)
    yc = y - mu
    var = jnp.mean(yc * yc, axis=0, keepdims=True)
    bonus = jnp.sum(r * k2 * rk, axis=0, keepdims=True) * v
    return yc * lax.rsqrt(var + GN_EPS) * lng + lnb + bonus


def _wkv_tok_body(r_ref, w_ref, k_ref, kk_ref, a_ref, v_ref, rk_ref, lng_ref, lnb_ref,
                  y_ref, st_ref, s_sc, y_sc, c_sc, yc_sc, *, tb, n):
    @pl.when(pl.program_id(1) == 0)
    def _():
        s_sc[...] = jnp.zeros_like(s_sc)

    ins = (r_ref, w_ref, k_ref, kk_ref, a_ref, v_ref)

    def to_chain(t, carry):
        for ai, ref in enumerate(ins):
            c_sc[ai, t] = ref[t].T
        return carry

    lax.fori_loop(0, tb, to_chain, 0)

    for par in range(2):
        rows = slice(par * n, (par + 1) * n)

        def step(t, carry, par=par, rows=rows):
            r, _, k2, kkr, a, v = [c_sc[ai, t, rows, :] for ai in range(len(ins))]
            which = {'r': 0, 'w': 1, 'k': 2}
            rows_of = lambda name, k: c_sc[which[name], t, par * n + k:par * n + k + 1, :]
            yc_sc[t, rows, :] = _wkv_step_kmajor(s_sc.at[par], y_sc, rows_of, r, k2, kkr, a, v,
                                                 rk_ref[par], lng_ref[par], lnb_ref[par], n)
            return carry

        lax.fori_loop(0, tb, step, 0)

    def from_chain(t, carry):
        y_ref[t] = yc_sc[t].T
        return carry

    lax.fori_loop(0, tb, from_chain, 0)

    @pl.when(pl.program_id(1) == pl.num_programs(1) - 1)
    def _():
        st_ref[0] = s_sc[...]


def _wkv_tok(r, w, k2, kk, a, v, rk, lng, lnb):
    t_len, rows, _ = r.shape
    n = RW_HEAD
    tb = _tile(t_len, (16, 8, 4, 2, 1))
    seqs = pl.BlockSpec((tb, LANES, LANES), lambda ci, ti: (ti, ci, 0))
    par = pl.BlockSpec((2, n, LANES), lambda ci, ti: (0, 0, 0))
    groups = rows // LANES
    return pl.pallas_call(
        functools.partial(_wkv_tok_body, tb=tb, n=n),
        grid=(groups, t_len // tb),
        in_specs=[seqs] * 6 + [par, par, par],
        out_specs=[seqs, pl.BlockSpec((1, 2, n, n, LANES), lambda ci, ti: (ci, 0, 0, 0, 0))],
        out_shape=[jax.ShapeDtypeStruct((t_len, rows, LANES), F32),
                   jax.ShapeDtypeStruct((groups, 2, n, n, LANES), F32)],
        scratch_shapes=[pltpu.VMEM((2, n, n, LANES), F32), pltpu.VMEM((2, n, LANES), F32),
                        pltpu.VMEM((6, tb, LANES, LANES), F32), pltpu.VMEM((tb, LANES, LANES), F32)],
        compiler_params=pltpu.CompilerParams(dimension_semantics=("parallel", "arbitrary")),
        name="wkv_tok",
    )(r, w, k2, kk, a, v, rk, lng, lnb)


def _wkv(r, w, k2, kk, a, v, s0, rk, lng, lnb):
    t_len, n, c = r.shape
    tb = _tile(t_len, (16, 8, 4, 2, 1))
    seqs = pl.BlockSpec((tb, n, LANES), lambda ci, ti: (ti, 0, ci))
    state = pl.BlockSpec((n, n, LANES), lambda ci, ti: (0, 0, ci))
    par = pl.BlockSpec((n, LANES), lambda ci, ti: (0, 0))
    return pl.pallas_call(
        functools.partial(_wkv_body, tb=tb, n=n),
        grid=(c // LANES, t_len // tb),
        in_specs=[seqs] * 6 + [state, par, par, par],
        out_specs=[seqs, state],
        out_shape=[jax.ShapeDtypeStruct((t_len, n, c), F32), jax.ShapeDtypeStruct((n, n, c), F32)],
        scratch_shapes=[pltpu.VMEM((n, n, LANES), F32), pltpu.VMEM((n, LANES), F32)],
        compiler_params=pltpu.CompilerParams(dimension_semantics=("parallel", "arbitrary")),
        name="wkv",
    )(r, w, k2, kk, a, v, s0, rk, lng, lnb)


def _wout_body(ga_ref, gb_ref, oa_ref, yb_ref, g_ref, x_ref, w_ref, lg_ref, lb_ref, h_ref, *, alpha):
    ob = yb_ref[...] * g_ref[...]
    merged = _sigmoid(ga_ref[...]) * oa_ref[...] + _sigmoid(gb_ref[...]) * ob
    acc = jnp.dot(merged.astype(BF16), w_ref[...], preferred_element_type=F32)
    h_ref[...] = _layer_norm(alpha * x_ref[...] + acc, lg_ref[...], lb_ref[...])


def _merge_wout(pa, o_a, y_b, g, x, w_out, ln_g, ln_b, alpha, t_len):
    m, d = x.shape
    time_major = y_b.shape[0] != m
    tm = _tile(t_len if time_major else m, (256, 128))
    row = pl.BlockSpec((tm, d), lambda i: (i, 0))
    vec = pl.BlockSpec((1, d), lambda i: (0, 0))
    nt = t_len // tm if time_major else 1
    y_spec = pl.BlockSpec((tm, d), lambda i: (i % nt, i // nt)) if time_major else row
    return pl.pallas_call(
        functools.partial(_wout_body, alpha=alpha),
        grid=(m // tm,),
        in_specs=[pl.BlockSpec((tm, d), lambda i: (i, 3)), pl.BlockSpec((tm, d), lambda i: (i, 4)), row, y_spec, row, row,
                  pl.BlockSpec((d, d), lambda i: (0, 0)), vec, vec],
        out_specs=row,
        out_shape=jax.ShapeDtypeStruct((m, d), F32),
        compiler_params=pltpu.CompilerParams(dimension_semantics=("parallel",)),
        name="merge_wout",
    )(pa, pa, o_a, y_b, g, x, w_out, ln_g, ln_b)


def _ffn_body(h_ref, wu_ref, wd_ref, lg_ref, lb_ref, y_ref, hb_sc, acc_sc, *, alpha):
    j = pl.program_id(1)

    @pl.when(j == 0)
    def _():
        hb_sc[...] = h_ref[...].astype(BF16)
        acc_sc[...] = jnp.zeros_like(acc_sc)

    u = jnp.dot(hb_sc[...], wu_ref[...], preferred_element_type=F32)
    u = jnp.square(jnp.maximum(u, 0.0)).astype(BF16)
    acc_sc[...] += jnp.dot(u, wd_ref[...], preferred_element_type=F32)

    @pl.when(j == pl.num_programs(1) - 1)
    def _():
        y_ref[...] = _layer_norm(alpha * h_ref[...] + acc_sc[...], lg_ref[...], lb_ref[...])


def _ffn(h, w_up, w_down, ln_g, ln_b, alpha, tf=512):
    m, d = h.shape
    ff = w_up.shape[1]
    tm = _tile(m, (1024, 512, 256, 128))
    vec = pl.BlockSpec((1, d), lambda i, j: (0, 0))
    return pl.pallas_call(
        functools.partial(_ffn_body, alpha=alpha),
        grid=(m // tm, ff // tf),
        in_specs=[pl.BlockSpec((tm, d), lambda i, j: (i, 0)), pl.BlockSpec((d, tf), lambda i, j: (0, j)),
                  pl.BlockSpec((tf, d), lambda i, j: (j, 0)), vec, vec],
        out_specs=pl.BlockSpec((tm, d), lambda i, j: (i, 0)),
        out_shape=jax.ShapeDtypeStruct((m, d), F32),
        scratch_shapes=[pltpu.VMEM((tm, d), BF16), pltpu.VMEM((tm, d), F32)],
        compiler_params=pltpu.CompilerParams(dimension_semantics=("parallel", "arbitrary")),
        name="ffn",
    )(h, w_up, w_down, ln_g, ln_b)


def _rw_pad(a, d):
    o = 3 * d
    parts = [a[..., :o]]
    for width, padded in zip((DECAY_LORA, AAA_LORA, GATE_LORA), LORA_PAD):
        seg = a[..., o:o + width]
        parts.append(jnp.pad(seg, [(0, 0)] * (a.ndim - 1) + [(0, padded - width)]))
        o += width
    return jnp.concatenate(parts, axis=-1)


def _rw_unpad(a, d):
    o = 3 * d
    parts = [a[..., :o]]
    for width, padded in zip((DECAY_LORA, AAA_LORA, GATE_LORA), LORA_PAD):
        parts.append(a[..., o:o + width])
        o += padded
    return jnp.concatenate(parts, axis=-1)


def _pad_rows(a, rows):
    return jnp.pad(a, ((0, rows - a.shape[0]), (0, 0)))


def _to_chain(a, b, t_len, heads):
    n = a.shape[1] // heads
    return a.reshape(b, t_len, heads, n).transpose(1, 3, 0, 2).reshape(t_len, n, b * heads)


def _from_chain(a, b, heads):
    t_len, n, _ = a.shape
    return a.reshape(t_len, n, b, heads).transpose(2, 0, 3, 1).reshape(b * t_len, heads * n)


def _layer(layer, x, b, t_len, pos0, attend, wkv0, shift0, lam_init, alpha, lp):
    m, d = x.shape
    rw_heads = d // RW_HEAD
    pa = _proj(x, lp['w_a'])
    p_rw = _proj(x, lp['w_b'])
    q_bf, k_rot, k_bf, v, v_bf = _rope(pa, d, t_len, pos0)
    o_a = attend(q_bf, k_rot, k_bf, v, v_bf)

    r, w, k2, kk, a, vr, g = _rwkv_prep(p_rw, shift0, d, t_len, lp['rw_mu'], lp['rw_w0'], lp['rw_w2'], lp['rw_a0'],
                                        lp['rw_a2'], lp['rw_g2'], lp['rw_k_k'], lp['rw_k_a'])
    if t_len > 1:
        rows = lambda z: z.reshape(t_len, b * d // LANES, LANES)
        y_r, s_t = _wkv_tok(rows(r), rows(w), rows(k2), rows(kk), rows(a), rows(vr),
                            lp['rk_t'], lp['lng_t'], lp['lnb_t'])
        y_b = y_r.reshape(t_len, b * d)
        pairs = rw_heads // 2
        wkv = (s_t.reshape(-1, 2, RW_HEAD, RW_HEAD, LANES // pairs, pairs)
               .transpose(0, 4, 5, 1, 3, 2).reshape(b, rw_heads, RW_HEAD, RW_HEAD))
    else:
        chain = lambda z: _to_chain(z, b, t_len, rw_heads)
        s0 = wkv0.transpose(2, 3, 0, 1).reshape(RW_HEAD, RW_HEAD, b * rw_heads)
        y_c, s_t = _wkv(chain(r), chain(w), chain(k2), chain(kk), chain(a), chain(vr), s0,
                        lp['rk_c'], lp['lng_c'], lp['lnb_c'])
        y_b = _from_chain(y_c, b, rw_heads)
        wkv = s_t.reshape(RW_HEAD, RW_HEAD, b, rw_heads).transpose(2, 3, 0, 1)
    shift = _rw_unpad(p_rw.reshape(b, t_len, -1)[:, -1], d)

    h = _merge_wout(pa, o_a, y_b, g, x, lp['w_out'], lp['ln1_g'], lp['ln1_b'], alpha, t_len)
    y = _ffn(h, lp['w_up'], lp['w_down'], lp['ln2_g'], lp['ln2_b'], alpha)
    return y, k_rot, v, wkv, shift


def kernel(x_prompt, x_sample, cache_k, cache_v, state_wkv, state_shift, page_table, w_in, lam_q1, lam_k1, lam_q2, lam_k2, da_norm_g, rw_mu, rw_w0, rw_w2, rw_a0, rw_a2, rw_g2, rw_k_k, rw_k_a, rw_r_k, rw_ln_g, rw_ln_b, w_out, ln1_g, ln1_b, w_up, w_down, ln2_g, ln2_b):
    b, t_len, d = x_prompt.shape
    bd, td, _ = x_sample.shape
    assert td == 1, "decode path handles one new token per sample"
    depth = w_in.shape[0]
    page = cache_k.shape[2]
    heads = cache_k.shape[3]
    assert cache_k.shape[4] == LANES and cache_v.shape[4] == LANES and heads * LANES == d
    assert t_len % page == 0
    past_len = page_table.shape[1] * page
    rw_heads = d // RW_HEAD
    alpha = (2 * depth) ** 0.25
    o_rw = 5 * d

    yp = x_prompt.reshape(b * t_len, d)
    ys = x_sample.reshape(bd, d)
    outs = [[] for _ in range(8)]
    for l in range(depth):
        lam_init = 0.8 - 0.6 * math.exp(-0.3 * l)
        row = lambda a: a[l].reshape(1, -1)
        lamv = _pad_rows(jnp.pad(jnp.stack([lam_q1[l], lam_k1[l], lam_q2[l], lam_k2[l]]), ((0, 0), (0, LANES - DA_QK))), 8)
        lane_heads = lambda a: jnp.tile(a.reshape(rw_heads, RW_HEAD).T, (1, LANES // rw_heads))
        tok_head = np.array([[2 * (ln % (rw_heads // 2)) + par for ln in range(LANES)] for par in range(2)])
        tok_lane_heads = lambda a: jnp.stack([a.reshape(rw_heads, RW_HEAD).T[:, tok_head[par]] for par in range(2)])
        lp = {
            'w_a': w_in[l][:, :o_rw].astype(BF16),
            'w_b': _rw_pad(w_in[l][:, o_rw:], d).astype(BF16),
            'rw_mu': _rw_pad(row(rw_mu), d),
            'rw_w0': row(rw_w0), 'rw_a0': row(rw_a0), 'rw_k_k': row(rw_k_k), 'rw_k_a': row(rw_k_a),
            'rw_w2': _pad_rows(rw_w2[l], LORA_PAD[0]).astype(BF16),
            'rw_a2': _pad_rows(rw_a2[l], LORA_PAD[1]).astype(BF16),
            'rw_g2': _pad_rows(rw_g2[l], LORA_PAD[2]).astype(BF16),
            'rk_c': lane_heads(rw_r_k[l]), 'lng_c': lane_heads(rw_ln_g[l]), 'lnb_c': lane_heads(rw_ln_b[l]),
            'rk_t': tok_lane_heads(rw_r_k[l]), 'lng_t': tok_lane_heads(rw_ln_g[l]), 'lnb_t': tok_lane_heads(rw_ln_b[l]),
            'w_out': w_out[l].astype(BF16), 'ln1_g': row(ln1_g), 'ln1_b': row(ln1_b),
            'w_up': w_up[l].astype(BF16), 'w_down': w_down[l].astype(BF16), 'ln2_g': row(ln2_g), 'ln2_b': row(ln2_b),
        }
        g_da = row(da_norm_g)

        def attend_p(q_bf, k_rot, k_bf, v, v_bf):
            return _attn_prompt(lamv, q_bf, k_bf, v_bf, g_da, b, t_len, heads, lam_init)

        def attend_s(q_bf, k_rot, k_bf, v, v_bf):
            q = q_bf.astype(F32).reshape(bd, heads, LANES)
            lane = jnp.arange(LANES)
            q16 = jnp.concatenate([jnp.where(lane < DA_QK, q, 0.0), jnp.where(lane >= DA_QK, q, 0.0)], axis=1)
            twice = lambda z: jnp.tile(z.reshape(bd, heads, LANES), (1, 2, 1))
            pool = cache_k.shape[1]
            o = _attn_decode(l, lamv, q16, twice(k_rot), twice(v), g_da.reshape(heads, LANES),
                             cache_k.reshape(depth, pool, page * heads, LANES),
                             cache_v.reshape(depth, pool, page * heads, LANES), page_table, lam_init)
            return o.reshape(bd, d)

        wkv0_p = jnp.zeros((b, rw_heads, RW_HEAD, RW_HEAD), F32)
        yp, kp, vp, wp, hp = _layer(l, yp, b, t_len, 0, attend_p, wkv0_p, None, lam_init, alpha, lp)
        ys, ks, vs, ws, hs = _layer(l, ys, bd, 1, past_len, attend_s, state_wkv[l], _rw_pad(state_shift[l], d),
                                    lam_init, alpha, lp)
        outs[0].append(kp.reshape(b * t_len // page, page, heads, LANES))
        outs[1].append(vp.reshape(b * t_len // page, page, heads, LANES))
        outs[2].append(ks.reshape(bd, 1, heads, LANES))
        outs[3].append(vs.reshape(bd, 1, heads, LANES))
        outs[4].append(wp)
        outs[5].append(ws)
        outs[6].append(hp)
        outs[7].append(hs)
    return (yp.reshape(b, t_len, d), ys.reshape(bd, 1, d)) + tuple(jnp.stack(o) for o in outs)
```
